```python
import math, functools
import jax, jax.numpy as jnp
from jax import lax
import numpy as np

D_MODEL = 1024
BATCH = 16
SEQ = 2048
DEPTH = 1
DEC_BATCH = 128
DEC_SEQ = 4
PAST_LEN = 8192
PAGE_SIZE = 128

DA_HEADS = 8
DA_DH = 64
DA_DV = 2 * DA_DH
ROT_DIM = DA_DH // 4
ROPE_THETA = 500000.0
Q_BLOCK = 128
POOL_WINDOWS = (2, 4, 8, 16)
POOL_GROUP = 128
POOL_W = POOL_GROUP * len(POOL_WINDOWS)
POOL_HIST = max(POOL_WINDOWS) - 1
MEM_LEN = 256
XA_HEADS = 4
XA_DH = 128
XA_W = XA_HEADS * XA_DH
N_BRANCH = 3
D_FF = 2816
DA_QK_W = DA_HEADS * 2 * DA_DH
DA_V_W = DA_HEADS * DA_DV
IN_COLS = POOL_W + 2 * DA_QK_W + DA_V_W + XA_W + N_BRANCH * D_MODEL
NORM_EPS = 1e-6
SUBLN_EPS = 1e-5
NEG_INF = -1e30

kernel_name = 'hybrid_pool_diffattn_memxattn_macaron_step'


def rms_norm(x, g, eps=NORM_EPS):
    xf = x.astype(jnp.float32)
    y = xf * lax.rsqrt(jnp.mean(xf * xf, axis=-1, keepdims=True) + eps)
    return (y * g.astype(jnp.float32)).astype(x.dtype)


def swiglu(h, wg, wu, wd):
    return (jax.nn.silu(h @ wg) * (h @ wu)) @ wd


def rope_partial(t, pos):
    half = ROT_DIM // 2
    inv = ROPE_THETA ** (-jnp.arange(half, dtype=jnp.float32) * 2.0 / ROT_DIM)
    ang = pos.astype(jnp.float32)[:, None] * inv[None, :]
    cos = jnp.cos(ang)[None, :, None, None, :]
    sin = jnp.sin(ang)[None, :, None, None, :]
    tf = t.astype(jnp.float32)
    x1, x2 = tf[..., :half], tf[..., half:ROT_DIM]
    out = jnp.concatenate([x1 * cos - x2 * sin, x2 * cos + x1 * sin, tf[..., ROT_DIM:]], axis=-1)
    return out.astype(t.dtype)


def diff_lambda(lq1, lk1, lq2, lk2, lam_init):
    f = jnp.float32
    return (jnp.exp(jnp.sum(lq1.astype(f) * lk1.astype(f)))
            - jnp.exp(jnp.sum(lq2.astype(f) * lk2.astype(f))) + lam_init)


def diff_attn_prompt(q, k, v, lam):
    B, S = q.shape[0], q.shape[1]
    nqb = S // Q_BLOCK
    qb = q.reshape(B, nqb, Q_BLOCK, DA_HEADS, 2, DA_DH).transpose(1, 0, 2, 3, 4, 5)
    kpos = jnp.arange(S)
    vf = v.astype(jnp.float32)
    scale = DA_DH ** -0.5

    def block(args):
        i, qi = args
        s = jnp.einsum('bqhmd,bkhmd->bhmqk', qi, k, preferred_element_type=jnp.float32) * scale
        qpos = i * Q_BLOCK + jnp.arange(Q_BLOCK)
        s = jnp.where(kpos[None, :] <= qpos[:, None], s, NEG_INF)
        p = jax.nn.softmax(s, axis=-1)
        w = p[:, :, 0] - lam * p[:, :, 1]
        return jnp.einsum('bhqk,bkhv->bqhv', w, vf)

    o = lax.map(block, (jnp.arange(nqb), qb))
    return o.transpose(1, 0, 2, 3, 4).reshape(B, S, DA_HEADS, DA_DV)


def diff_attn_sample(q, k_new, v_new, lam, cache_k, cache_v, layer, page_table):
    Bd, T = q.shape[0], q.shape[1]
    qf = q.astype(jnp.float32)
    scale = DA_DH ** -0.5

    def update(carry, k_blk, v_blk, mask):
        m, l, acc = carry
        s = jnp.einsum('bqhmd,bkhmd->bhmqk', qf, k_blk.astype(jnp.float32)) * scale
        if mask is not None:
            s = jnp.where(mask, s, NEG_INF)
        m_new = jnp.maximum(m, jnp.max(s, axis=-1))
        a = jnp.exp(m - m_new)
        p = jnp.exp(s - m_new[..., None])
        l = l * a + jnp.sum(p, axis=-1)
        acc = acc * a[..., None] + jnp.einsum('bhmqk,bkhv->bhmqv', p, v_blk.astype(jnp.float32))
        return (m_new, l, acc)

    def step(carry, phys):
        kp = cache_k[layer, phys].reshape(Bd, PAGE_SIZE, DA_HEADS, 2, DA_DH)
        vp = cache_v[layer, phys]
        return update(carry, kp, vp, None), None

    init = (jnp.full((Bd, DA_HEADS, 2, T), NEG_INF, jnp.float32),
            jnp.zeros((Bd, DA_HEADS, 2, T), jnp.float32),
            jnp.zeros((Bd, DA_HEADS, 2, T, DA_DV), jnp.float32))
    carry, _ = lax.scan(step, init, page_table.T)
    causal = jnp.arange(T)[None, :] <= jnp.arange(T)[:, None]
    m, l, acc = update(carry, k_new, v_new, causal)
    o = acc / l[..., None]
    out = o[:, :, 0] - lam * o[:, :, 1]
    return out.transpose(0, 2, 1, 3)


def pool_mix(u_ext, n_prefix, start, pool_w, pool_scale):
    B, Lx, C = u_ext.shape
    uf = u_ext.astype(jnp.float32)
    cz = jnp.concatenate([jnp.zeros((B, 1, C), jnp.float32), lax.cumsum(uf, axis=1)], axis=1)
    row_pos = start - n_prefix + jnp.arange(Lx)
    means = []
    for g, w in enumerate(POOL_WINDOWS):
        czg = cz[..., g * POOL_GROUP:(g + 1) * POOL_GROUP]
        lag = jnp.pad(czg, ((0, 0), (w, 0), (0, 0)))[:, :Lx + 1]
        cnt = jnp.clip(row_pos + 1, 1, w).astype(jnp.float32)
        means.append((czg[:, 1:] - lag[:, 1:]) / cnt[None, :, None])
    L = Lx - n_prefix
    d = (jnp.concatenate(means, axis=-1) - uf)[:, n_prefix:]
    d = d.reshape(B, L, len(POOL_WINDOWS), POOL_GROUP)
    y = jnp.einsum('blgc,gcd->blgd', d, pool_w.astype(jnp.float32)).reshape(B, L, POOL_W)
    return (y * pool_scale.astype(jnp.float32)).astype(u_ext.dtype)


def cross_attend(q, mk, mv):
    B, L = q.shape[0], q.shape[1]
    s = jnp.einsum('blhd,bmhd->bhlm', q, mk, preferred_element_type=jnp.float32) * (XA_DH ** -0.5)
    p = jax.nn.softmax(s, axis=-1)
    o = jnp.einsum('bhlm,bmhd->blhd', p, mv.astype(jnp.float32))
    return o.reshape(B, L, XA_W).astype(q.dtype)


def decoder_layer(x, start, pool_hist, mem_k, mem_v, attn_fn, lam_init, lw):
    B, L, _ = x.shape
    dt = x.dtype
    x = x + 0.5 * swiglu(rms_norm(x, lw['ffn1_norm']), lw['ffn1_w_gate'], lw['ffn1_w_up'], lw['ffn1_w_down'])
    h = rms_norm(x, lw['mix_norm'])
    z = h @ lw['w_in']
    o0 = POOL_W
    o1 = o0 + DA_QK_W
    o2 = o1 + DA_QK_W
    o3 = o2 + DA_V_W
    o4 = o3 + XA_W
    pos = start + jnp.arange(L)
    u_pool = z[..., :o0]
    q = rope_partial(z[..., o0:o1].reshape(B, L, DA_HEADS, 2, DA_DH), pos)
    k = rope_partial(z[..., o1:o2].reshape(B, L, DA_HEADS, 2, DA_DH), pos)
    v = z[..., o2:o3].reshape(B, L, DA_HEADS, DA_DV)
    xq = z[..., o3:o4].reshape(B, L, XA_HEADS, XA_DH)
    gates = jax.nn.sigmoid(z[..., o4:].reshape(B, L, N_BRANCH, D_MODEL))
    u_ext = u_pool if pool_hist is None else jnp.concatenate([pool_hist.astype(dt), u_pool], axis=1)
    y_pool = pool_mix(u_ext, u_ext.shape[1] - L, start, lw['pool_w'], lw['pool_scale'])
    lam = diff_lambda(lw['lambda_q1'], lw['lambda_k1'], lw['lambda_q2'], lw['lambda_k2'], lam_init)
    o = attn_fn(q, k, v, lam)
    o = rms_norm(o, lw['subln_gain'], SUBLN_EPS) * (1.0 - lam_init)
    y_diff = o.reshape(B, L, DA_V_W).astype(dt)
    y_mem = cross_attend(xq, mem_k, mem_v)
    merged = (gates[:, :, 0] * (y_pool @ lw['w_br_pool'])
              + gates[:, :, 1] * (y_diff @ lw['w_br_diff'])
              + gates[:, :, 2] * (y_mem @ lw['w_br_mem']))
    x = x + merged @ lw['w_out']
    x = x + 0.5 * swiglu(rms_norm(x, lw['ffn2_norm']), lw['ffn2_w_gate'], lw['ffn2_w_up'], lw['ffn2_w_down'])
    return x, k.reshape(B, L, DA_HEADS, 2 * DA_DH), v, u_ext[:, -POOL_HIST:]


def setup_inputs(seed: int = 0) -> dict:
    key = jax.random.key(seed)
    ks = jax.random.split(key, 40)
    f32 = jnp.float32
    n_pages = PAST_LEN // PAGE_SIZE
    n_used = DEC_BATCH * n_pages
    n_pool = n_used + max(1, n_used // 4)

    def nrm(i, shape, scale=1.0):
        return jax.random.normal(ks[i], shape, f32) * scale

    def gain(i, shape):
        return 1.0 + nrm(i, shape, 0.02)

    page_table = jax.random.permutation(ks[0], n_pool)[:n_used].reshape(DEC_BATCH, n_pages).astype(jnp.int32)
    return {
        'x_prompt': nrm(1, (BATCH, SEQ, D_MODEL)),
        'x_sample': nrm(2, (DEC_BATCH, DEC_SEQ, D_MODEL)),
        'mem_prompt': nrm(3, (BATCH, MEM_LEN, D_MODEL)),
        'cache_k': nrm(4, (DEPTH, n_pool, PAGE_SIZE, DA_HEADS, 2 * DA_DH)),
        'cache_v': nrm(5, (DEPTH, n_pool, PAGE_SIZE, DA_HEADS, DA_DV)),
        'state_pool': nrm(6, (DEPTH, DEC_BATCH, POOL_HIST, POOL_W)),
        'cache_mem_k': nrm(7, (DEPTH, DEC_BATCH, MEM_LEN, XA_HEADS, XA_DH)),
        'cache_mem_v': nrm(8, (DEPTH, DEC_BATCH, MEM_LEN, XA_HEADS, XA_DH)),
        'page_table': page_table,
        'ffn1_norm': gain(9, (DEPTH, D_MODEL)),
        'ffn1_w_gate': nrm(10, (DEPTH, D_MODEL, D_FF), D_MODEL ** -0.5),
        'ffn1_w_up': nrm(11, (DEPTH, D_MODEL, D_FF), D_MODEL ** -0.5),
        'ffn1_w_down': nrm(12, (DEPTH, D_FF, D_MODEL), D_FF ** -0.5),
        'mix_norm': gain(13, (DEPTH, D_MODEL)),
        'w_in': nrm(14, (DEPTH, D_MODEL, IN_COLS), D_MODEL ** -0.5),
        'pool_w': nrm(15, (DEPTH, len(POOL_WINDOWS), POOL_GROUP, POOL_GROUP), POOL_GROUP ** -0.5),
        'pool_scale': gain(16, (DEPTH, POOL_W)),
        'lambda_q1': nrm(17, (DEPTH, DA_DH), 0.1),
        'lambda_k1': nrm(18, (DEPTH, DA_DH), 0.1),
        'lambda_q2': nrm(19, (DEPTH, DA_DH), 0.1),
        'lambda_k2': nrm(20, (DEPTH, DA_DH), 0.1),
        'subln_gain': gain(21, (DEPTH, DA_DV)),
        'mem_norm': gain(22, (DEPTH, D_MODEL)),
        'w_mem_kv': nrm(23, (DEPTH, D_MODEL, 2 * XA_W), D_MODEL ** -0.5),
        'w_br_pool': nrm(24, (DEPTH, POOL_W, D_MODEL), POOL_W ** -0.5),
        'w_br_diff': nrm(25, (DEPTH, DA_V_W, D_MODEL), DA_V_W ** -0.5),
        'w_br_mem': nrm(26, (DEPTH, XA_W, D_MODEL), XA_W ** -0.5),
        'w_out': nrm(27, (DEPTH, D_MODEL, D_MODEL), D_MODEL ** -0.5),
        'ffn2_norm': gain(28, (DEPTH, D_MODEL)),
        'ffn2_w_gate': nrm(29, (DEPTH, D_MODEL, D_FF), D_MODEL ** -0.5),
        'ffn2_w_up': nrm(30, (DEPTH, D_MODEL, D_FF), D_MODEL ** -0.5),
        'ffn2_w_down': nrm(31, (DEPTH, D_FF, D_MODEL), D_FF ** -0.5),
        'final_norm': gain(32, (D_MODEL,)),
    }


def reference(x_prompt, x_sample, mem_prompt, cache_k, cache_v, state_pool, cache_mem_k, cache_mem_v,
              page_table, ffn1_norm, ffn1_w_gate, ffn1_w_up, ffn1_w_down, mix_norm, w_in, pool_w,
              pool_scale, lambda_q1, lambda_k1, lambda_q2, lambda_k2, subln_gain, mem_norm, w_mem_kv,
              w_br_pool, w_br_diff, w_br_mem, w_out, ffn2_norm, ffn2_w_gate, ffn2_w_up, ffn2_w_down,
              final_norm):
    past_len = page_table.shape[1] * cache_k.shape[2]
    Bp, M = mem_prompt.shape[0], mem_prompt.shape[1]
    xp, xs = x_prompt, x_sample
    kp_l, vp_l, tp_l, mkp_l, mvp_l, ks_l, vs_l, ts_l = [], [], [], [], [], [], [], []
    for li in range(DEPTH):
        lw = {
            'ffn1_norm': ffn1_norm[li], 'ffn1_w_gate': ffn1_w_gate[li], 'ffn1_w_up': ffn1_w_up[li],
            'ffn1_w_down': ffn1_w_down[li], 'mix_norm': mix_norm[li], 'w_in': w_in[li],
            'pool_w': pool_w[li], 'pool_scale': pool_scale[li], 'lambda_q1': lambda_q1[li],
            'lambda_k1': lambda_k1[li], 'lambda_q2': lambda_q2[li], 'lambda_k2': lambda_k2[li],
            'subln_gain': subln_gain[li], 'w_br_pool': w_br_pool[li], 'w_br_diff': w_br_diff[li],
            'w_br_mem': w_br_mem[li], 'w_out': w_out[li], 'ffn2_norm': ffn2_norm[li],
            'ffn2_w_gate': ffn2_w_gate[li], 'ffn2_w_up': ffn2_w_up[li], 'ffn2_w_down': ffn2_w_down[li],
        }
        lam_init = 0.8 - 0.6 * math.exp(-0.3 * li)
        mkv = rms_norm(mem_prompt, mem_norm[li]) @ w_mem_kv[li]
        mk = mkv[..., :XA_W].reshape(Bp, M, XA_HEADS, XA_DH)
        mv = mkv[..., XA_W:].reshape(Bp, M, XA_HEADS, XA_DH)
        xp, kp, vp, tp = decoder_layer(xp, 0, None, mk, mv, diff_attn_prompt, lam_init, lw)
        attn_s = functools.partial(diff_attn_sample, cache_k=cache_k, cache_v=cache_v, layer=li,
                                   page_table=page_table)
        xs, ks, vs, ts = decoder_layer(xs, past_len, state_pool[li], cache_mem_k[li], cache_mem_v[li],
                                       attn_s, lam_init, lw)
        kp_l.append(kp); vp_l.append(vp); tp_l.append(tp); mkp_l.append(mk); mvp_l.append(mv)
        ks_l.append(ks); vs_l.append(vs); ts_l.append(ts)
    y_prompt = rms_norm(xp, final_norm)
    y_sample = rms_norm(xs, final_norm)
    k_prompt = jnp.stack(kp_l)
    v_prompt = jnp.stack(vp_l)
    pool_prompt = jnp.stack(tp_l)
    mem_k_prompt = jnp.stack(mkp_l)
    mem_v_prompt = jnp.stack(mvp_l)
    k_sample = jnp.stack(ks_l)
    v_sample = jnp.stack(vs_l)
    pool_sample = jnp.stack(ts_l)
    return (y_prompt, y_sample, k_prompt, v_prompt, pool_prompt, mem_k_prompt, mem_v_prompt, k_sample, v_sample, pool_sample)
```

```python
import functools
import math

import jax
import jax.numpy as jnp
from jax import lax
from jax.experimental import pallas as pl
from jax.experimental.pallas import tpu as pltpu

F32 = jnp.float32
BF16 = jnp.bfloat16

D_MODEL = 1024
DA_HEADS = 8
DA_DH = 64
DA_DV = 2 * DA_DH
HEAD_W = 2 * DA_DH
ROT_DIM = DA_DH // 4
ROPE_THETA = 500000.0
POOL_WINDOWS = (2, 4, 8, 16)
POOL_GROUP = 128
POOL_W = POOL_GROUP * len(POOL_WINDOWS)
POOL_HIST = max(POOL_WINDOWS) - 1
XA_HEADS = 4
XA_DH = 128
XA_W = XA_HEADS * XA_DH
N_BRANCH = 3
D_FF = 2816
DA_QK_W = DA_HEADS * 2 * DA_DH
DA_V_W = DA_HEADS * DA_DV
NORM_EPS = 1e-6
SUBLN_EPS = 1e-5
NEG_INF = -1e30

VMEM_LIMIT_BYTES = 56 * 1024 * 1024
TOKEN_TILE = 512
FF_CHUNK = 256
ATTN_BQ = 256
PAGES_PER_STEP = 4
POOL_PAD = 16


def _cparams(sem):
    return pltpu.CompilerParams(dimension_semantics=sem, vmem_limit_bytes=VMEM_LIMIT_BYTES)


def _resident(shape):
    nd = len(shape)
    return pl.BlockSpec(shape, lambda *_: (0,) * nd, pipeline_mode=pl.Buffered(1))


def _rms_bf16(x, g):
    ms = jnp.mean(x * x, axis=-1, keepdims=True)
    return (x * lax.rsqrt(ms + NORM_EPS) * g).astype(BF16)


def _sigmoid(z):
    return 1.0 / (1.0 + jnp.exp(-z))


def _ffn_kernel(x_ref, g_ref, wg_ref, wu_ref, wd_ref, *rest, final):
    if final:
        fn_ref, o_ref, h_scr, a_scr = rest
    else:
        o_ref, h_scr, a_scr = rest
    h_scr[...] = _rms_bf16(x_ref[...], g_ref[...])
    for c in range(D_FF // FF_CHUNK):
        sl = slice(c * FF_CHUNK, (c + 1) * FF_CHUNK)
        hb = h_scr[...]
        g = jnp.dot(hb, wg_ref[:, sl], preferred_element_type=F32)
        u = jnp.dot(hb, wu_ref[:, sl], preferred_element_type=F32)
        a_scr[:, sl] = (g * _sigmoid(g) * u).astype(BF16)
    y = jnp.dot(a_scr[...], wd_ref[...], preferred_element_type=F32)
    x2 = x_ref[...] + 0.5 * y
    if final:
        ms = jnp.mean(x2 * x2, axis=-1, keepdims=True)
        x2 = x2 * lax.rsqrt(ms + NORM_EPS) * fn_ref[...]
    o_ref[...] = x2


def _ffn(x, gain, wg, wu, wd, final_gain=None):
    t = x.shape[0]
    tm = min(TOKEN_TILE, t)
    final = final_gain is not None
    row = pl.BlockSpec((tm, D_MODEL), lambda i: (i, 0))
    in_specs = [row, _resident((1, D_MODEL)), _resident((D_MODEL, D_FF)), _resident((D_MODEL, D_FF)),
                _resident((D_FF, D_MODEL))]
    args = [x, gain, wg, wu, wd]
    if final:
        in_specs.append(_resident((1, D_MODEL)))
        args.append(final_gain)
    return pl.pallas_call(
        functools.partial(_ffn_kernel, final=final),
        grid=(t // tm,),
        in_specs=in_specs,
        out_specs=row,
        out_shape=jax.ShapeDtypeStruct((t, D_MODEL), F32),
        scratch_shapes=[pltpu.VMEM((tm, D_MODEL), BF16), pltpu.VMEM((tm, D_FF), BF16)],
        compiler_params=_cparams(("arbitrary",)),
        name="ffn_final" if final else "ffn",
    )(*args)


def _rope(z, c, a, b):
    return z * c + pltpu.roll(z, HEAD_W - ROT_DIM // 2, 1) * a + pltpu.roll(z, ROT_DIM // 2, 1) * b


def _inproj_kernel(x_ref, g_ref, w_ref, c_ref, a_ref, b_ref, u_ref, q_ref, k_ref, v_ref, xq_ref, h_scr):
    h_scr[...] = _rms_bf16(x_ref[...], g_ref[...])
    o0 = POOL_W
    o1 = o0 + DA_QK_W
    o2 = o1 + DA_QK_W
    o3 = o2 + DA_V_W
    o4 = o3 + XA_W
    u_ref[...] = jnp.dot(h_scr[...], w_ref[:, :o0], preferred_element_type=F32)
    c, a, b = c_ref[...], a_ref[...], b_ref[...]
    zq = jnp.dot(h_scr[...], w_ref[:, o0:o1], preferred_element_type=F32)
    for hd in range(DA_HEADS):
        sl = slice(hd * HEAD_W, (hd + 1) * HEAD_W)
        q_ref[:, sl] = (_rope(zq[:, sl], c, a, b) * (DA_DH ** -0.5)).astype(q_ref.dtype)
    zk = jnp.dot(h_scr[...], w_ref[:, o1:o2], preferred_element_type=F32)
    for hd in range(DA_HEADS):
        sl = slice(hd * HEAD_W, (hd + 1) * HEAD_W)
        k_ref[:, sl] = _rope(zk[:, sl], c, a, b)
    v_ref[...] = jnp.dot(h_scr[...], w_ref[:, o2:o3], preferred_element_type=F32)
    xq_ref[...] = jnp.dot(h_scr[...], w_ref[:, o3:o4], preferred_element_type=F32)


def _inproj(x, gain, w, tabs, q_dtype):
    t = x.shape[0]
    tm = min(TOKEN_TILE, t)
    n_tab = tabs[0].shape[0] // tm
    ncol = w.shape[1]

    def row(wd):
        return pl.BlockSpec((tm, wd), lambda i: (i, 0))

    tab = pl.BlockSpec((tm, HEAD_W), lambda i: (i % n_tab, 0))
    return pl.pallas_call(
        _inproj_kernel,
        grid=(t // tm,),
        in_specs=[row(D_MODEL), _resident((1, D_MODEL)), _resident((D_MODEL, ncol)), tab, tab, tab],
        out_specs=[row(POOL_W), row(DA_QK_W), row(DA_QK_W), row(DA_V_W), row(XA_W)],
        out_shape=[jax.ShapeDtypeStruct((t, POOL_W), F32), jax.ShapeDtypeStruct((t, DA_QK_W), q_dtype),
                   jax.ShapeDtypeStruct((t, DA_QK_W), F32), jax.ShapeDtypeStruct((t, DA_V_W), F32),
                   jax.ShapeDtypeStruct((t, XA_W), F32)],
        scratch_shapes=[pltpu.VMEM((tm, D_MODEL), BF16)],
        compiler_params=_cparams(("arbitrary",)),
        name="inproj",
    )(x, gain, w, *tabs)


def _rope_tables(pos):
    half = ROT_DIM // 2
    inv = ROPE_THETA ** (-jnp.arange(half, dtype=F32) * 2.0 / ROT_DIM)
    ang = pos.astype(F32)[:, None] * inv[None, :]
    cos, sin = jnp.cos(ang), jnp.sin(ang)
    n = pos.shape[0]
    one = jnp.ones((n, DA_DH - ROT_DIM), F32)
    zero = jnp.zeros((n, DA_DH - ROT_DIM), F32)
    zh = jnp.zeros((n, half), F32)
    c = jnp.concatenate([cos, cos, one], axis=1)
    a = jnp.concatenate([-sin, zh, zero], axis=1)
    b = jnp.concatenate([zh, sin, zero], axis=1)
    return tuple(jnp.concatenate([m, m], axis=1) for m in (c, a, b))


def _pool_kernel(u_ref, pw_ref, ps_ref, y_ref, scr, *, lx, out_off, out_len, first_pos):
    scr[0:POOL_PAD, :] = jnp.zeros((POOL_PAD, POOL_W), F32)
    scr[POOL_PAD:POOL_PAD + lx, :] = u_ref[...]
    base = POOL_PAD + out_off
    pos = first_pos + out_off + lax.broadcasted_iota(jnp.int32, (out_len, 1), 0)
    for g, w in enumerate(POOL_WINDOWS):
        cs = slice(g * POOL_GROUP, (g + 1) * POOL_GROUP)
        acc = scr[base:base + out_len, cs]
        for j in range(1, w):
            acc = acc + scr[base - j:base - j + out_len, cs]
        cnt = jnp.clip(pos + 1, 1, w).astype(F32)
        d = acc / cnt - scr[base:base + out_len, cs]
        y = jnp.dot(d.astype(BF16), pw_ref[g], preferred_element_type=F32) * ps_ref[:, cs]
        y_ref[:, cs] = y.astype(y_ref.dtype)


def _pool(u_ext, pool_w, pool_scale, out_off, out_len, first_pos):
    nb, lx, _ = u_ext.shape
    return pl.pallas_call(
        functools.partial(_pool_kernel, lx=lx, out_off=out_off, out_len=out_len, first_pos=first_pos),
        grid=(nb,),
        in_specs=[pl.BlockSpec((None, lx, POOL_W), lambda i: (i, 0, 0)),
                  _resident((len(POOL_WINDOWS), POOL_GROUP, POOL_GROUP)), _resident((1, POOL_W))],
        out_specs=pl.BlockSpec((None, out_len, POOL_W), lambda i: (i, 0, 0)),
        out_shape=jax.ShapeDtypeStruct((nb, out_len, POOL_W), BF16),
        scratch_shapes=[pltpu.VMEM((POOL_PAD + lx, POOL_W), F32)],
        compiler_params=_cparams(("arbitrary",)),
        name="pool",
    )(u_ext, pool_w, pool_scale)


def _diff_lambda(lp_ref, lam_init):
    lp = lp_ref[...]
    s1 = jnp.sum(lp[0:1] * lp[1:2], axis=-1, keepdims=True)
    s2 = jnp.sum(lp[2:3] * lp[3:4], axis=-1, keepdims=True)
    return jnp.exp(s1) - jnp.exp(s2) + lam_init


def _subln(o, gain, lam_init):
    ms = jnp.mean(o * o, axis=-1, keepdims=True)
    return o * lax.rsqrt(ms + SUBLN_EPS) * gain * (1.0 - lam_init)


_NT = (((1,), (1,)), ((), ()))


def _attn_prompt_kernel(lp_ref, q_ref, k_ref, v_ref, gain_ref, o_ref, kb, vb, *, lam_init):
    qi = pl.program_id(2)
    bq = ATTN_BQ

    @pl.when(qi == 0)
    def _():
        kb[...] = k_ref[...].astype(BF16)
        vb[...] = v_ref[...].astype(BF16)

    q = q_ref[...]
    lane = lax.broadcasted_iota(jnp.int32, q.shape, 1)
    zero = jnp.zeros_like(q)
    qs = jnp.concatenate([jnp.where(lane < DA_DH, q, zero), jnp.where(lane >= DA_DH, q, zero)], axis=0)

    def block(j, carry, diag):
        m, l, acc = carry
        start = pl.multiple_of(j * bq, bq)
        s = lax.dot_general(qs, kb[pl.ds(start, bq), :], _NT, preferred_element_type=F32)
        if diag:
            r = lax.broadcasted_iota(jnp.int32, s.shape, 0)
            r = jnp.where(r >= bq, r - bq, r)
            col = lax.broadcasted_iota(jnp.int32, s.shape, 1)
            s = jnp.where(col <= r, s, NEG_INF)
        m_new = jnp.maximum(m, jnp.max(s, axis=-1, keepdims=True))
        alpha = jnp.exp(m - m_new)
        p = jnp.exp(s - m_new)
        l = alpha * l + jnp.sum(p, axis=-1, keepdims=True)
        acc = alpha * acc + jnp.dot(p.astype(BF16), vb[pl.ds(start, bq), :], preferred_element_type=F32)
        return m_new, l, acc

    init = (jnp.full((2 * bq, 1), NEG_INF, F32), jnp.zeros((2 * bq, 1), F32), jnp.zeros((2 * bq, DA_DV), F32))
    carry = lax.fori_loop(0, qi, lambda j, c: block(j, c, False), init)
    _, l, acc = block(qi, carry, True)
    o = acc / l
    lam = _diff_lambda(lp_ref, lam_init)
    out = o[:bq] - lam * o[bq:]
    o_ref[...] = _subln(out, gain_ref[...], lam_init).astype(o_ref.dtype)


def _attn_prompt(lam_par, q, k, v, gain, lam_init):
    nb, s, _ = q.shape
    bq = ATTN_BQ
    kv_spec = pl.BlockSpec((None, s, HEAD_W), lambda b, h, i: (b, 0, h))
    q_spec = pl.BlockSpec((None, bq, HEAD_W), lambda b, h, i: (b, i, h))
    return pl.pallas_call(
        functools.partial(_attn_prompt_kernel, lam_init=lam_init),
        grid=(nb, DA_HEADS, s // bq),
        in_specs=[_resident((8, HEAD_W)), q_spec, kv_spec, kv_spec, _resident((1, DA_DV))],
        out_specs=q_spec,
        out_shape=jax.ShapeDtypeStruct((nb, s, DA_V_W), BF16),
        scratch_shapes=[pltpu.VMEM((s, HEAD_W), BF16), pltpu.VMEM((s, DA_DV), BF16)],
        compiler_params=_cparams(("arbitrary", "arbitrary", "arbitrary")),
        name="attn_prompt",
    )(lam_par, q, k, v, gain)


def _attn_sample_kernel(pt_ref, lp_ref, wt_ref, kn_ref, vn_ref, gain_ref, ck_hbm, cv_hbm, o_ref,
                        kbuf, vbuf, sem, m_s, l_s, acc_s, *, lam_init, chunks_per_seq, page):
    c = pl.program_id(0)
    nchunk = pl.num_programs(0)
    ci = c % chunks_per_seq
    npg = PAGES_PER_STEP
    rows = acc_s.shape[0]

    def copies(cc, slot):
        out = []
        for p in range(npg):
            pg = pt_ref[cc * npg + p]
            dst = pl.ds(p * page * DA_HEADS, page * DA_HEADS)
            out.append(pltpu.make_async_copy(ck_hbm.at[pg], kbuf.at[slot, dst, :], sem.at[0, slot]))
            out.append(pltpu.make_async_copy(cv_hbm.at[pg], vbuf.at[slot, dst, :], sem.at[1, slot]))
        return out

    @pl.when(c == 0)
    def _():
        for d in copies(0, 0):
            d.start()

    @pl.when(c + 1 < nchunk)
    def _():
        for d in copies(c + 1, (c + 1) % 2):
            d.start()

    slot = c % 2
    for d in copies(c, slot):
        d.wait()

    @pl.when(ci == 0)
    def _():
        m_s[...] = jnp.full(m_s.shape, NEG_INF, F32)
        l_s[...] = jnp.zeros(l_s.shape, F32)
        acc_s[...] = jnp.zeros(acc_s.shape, F32)

    wt = wt_ref[...]

    def update(kblk, vblk, mask):
        s = lax.dot_general(wt, kblk, _NT, preferred_element_type=F32)
        if mask is not None:
            s = jnp.where(mask, s, NEG_INF)
        m_old = m_s[...]
        m_new = jnp.maximum(m_old, jnp.max(s, axis=-1, keepdims=True))
        alpha = jnp.exp(m_old - m_new)
        p = jnp.exp(s - m_new[:, :1])
        l_s[...] = alpha * l_s[...] + jnp.sum(p, axis=-1, keepdims=True)
        acc_s[...] = alpha[:, :1] * acc_s[...] + jnp.dot(p.astype(BF16), vblk, preferred_element_type=F32)
        m_s[...] = m_new

    keys = npg * page

    def heads_to_lanes(buf):
        ref = buf.at[slot]
        cols = [ref[pl.ds(hd, keys, stride=DA_HEADS), :] for hd in range(DA_HEADS)]
        return jnp.concatenate(cols, axis=1).astype(BF16)

    update(heads_to_lanes(kbuf), heads_to_lanes(vbuf), None)

    @pl.when(ci == chunks_per_seq - 1)
    def _():
        nt = o_ref.shape[0]
        nk = kn_ref.shape[0]
        r = lax.broadcasted_iota(jnp.int32, (rows, nk), 0)
        col = lax.broadcasted_iota(jnp.int32, (rows, nk), 1)
        update(kn_ref[...].astype(BF16), vn_ref[...].astype(BF16), col <= r % nt)
        o = acc_s[...] / l_s[...][:, :1]
        rh = lax.broadcasted_iota(jnp.int32, o.shape, 0) // (2 * nt)
        lh = lax.broadcasted_iota(jnp.int32, o.shape, 1) // DA_DV
        o = jnp.where(rh == lh, o, 0.0)
        red = jnp.sum(o.reshape(DA_HEADS, 2 * nt, DA_V_W), axis=0)
        lam = _diff_lambda(lp_ref, lam_init)
        out = red[:nt] - lam * red[nt:]
        gain = gain_ref[...]
        for hd in range(DA_HEADS):
            sl = slice(hd * DA_DV, (hd + 1) * DA_DV)
            o_ref[:, sl] = _subln(out[:, sl], gain, lam_init)


def _attn_sample(page_table, lam_par, wt, k_new, v_new, gain, cache_k, cache_v, lam_init):
    nb, n_pages = page_table.shape
    page = cache_k.shape[1] // DA_HEADS
    kw = DA_HEADS * cache_k.shape[2]
    rows = wt.shape[1]
    nt = rows // (2 * DA_HEADS)
    nk = k_new.shape[1]
    cps = n_pages // PAGES_PER_STEP
    keys = PAGES_PER_STEP * page
    grid_spec = pltpu.PrefetchScalarGridSpec(
        num_scalar_prefetch=1,
        grid=(nb * cps,),
        in_specs=[
            pl.BlockSpec((8, HEAD_W), lambda c, pt: (0, 0)),
            pl.BlockSpec((None, rows, kw), lambda c, pt: (c // cps, 0, 0)),
            pl.BlockSpec((None, nk, kw), lambda c, pt: (c // cps, 0, 0)),
            pl.BlockSpec((None, nk, kw), lambda c, pt: (c // cps, 0, 0)),
            pl.BlockSpec((1, DA_DV), lambda c, pt: (0, 0)),
            pl.BlockSpec(memory_space=pl.ANY),
            pl.BlockSpec(memory_space=pl.ANY),
        ],
        out_specs=pl.BlockSpec((None, nt, kw), lambda c, pt: (c // cps, 0, 0)),
        scratch_shapes=[
            pltpu.VMEM((2, keys * DA_HEADS, HEAD_W), F32), pltpu.VMEM((2, keys * DA_HEADS, DA_DV), F32),
            pltpu.SemaphoreType.DMA((2, 2)),
            pltpu.VMEM((rows, HEAD_W), F32), pltpu.VMEM((rows, HEAD_W), F32), pltpu.VMEM((rows, kw), F32),
        ],
    )
    return pl.pallas_call(
        functools.partial(_attn_sample_kernel, lam_init=lam_init, chunks_per_seq=cps, page=page),
        grid_spec=grid_spec,
        out_shape=jax.ShapeDtypeStruct((nb, nt, kw), F32),
        compiler_params=_cparams(("arbitrary",)),
        name="attn_sample",
    )(page_table.reshape(-1), lam_par, wt, k_new, v_new, gain, cache_k, cache_v)


def _memkv_kernel(x_ref, g_ref, w_ref, k_ref, v_ref):
    h = _rms_bf16(x_ref[...], g_ref[...])
    k_ref[...] = jnp.dot(h, w_ref[:, :XA_W], preferred_element_type=F32)
    v_ref[...] = jnp.dot(h, w_ref[:, XA_W:], preferred_element_type=F32)


def _memkv(x, gain, w):
    t = x.shape[0]
    tm = min(TOKEN_TILE, t)

    def row(wd):
        return pl.BlockSpec((tm, wd), lambda i: (i, 0))

    return pl.pallas_call(
        _memkv_kernel,
        grid=(t // tm,),
        in_specs=[row(D_MODEL), _resident((1, D_MODEL)), _resident((D_MODEL, 2 * XA_W))],
        out_specs=[row(XA_W), row(XA_W)],
        out_shape=[jax.ShapeDtypeStruct((t, XA_W), F32)] * 2,
        compiler_params=_cparams(("arbitrary",)),
        name="memkv",
    )(x, gain, w)


def _xattn_kernel(q_ref, mk_ref, mv_ref, o_ref, *, head_rows):
    q = q_ref[...].astype(BF16)
    for hd in range(XA_HEADS):
        sl = slice(hd * XA_DH, (hd + 1) * XA_DH)
        if head_rows:
            n = mk_ref.shape[0] // XA_HEADS
            mk = mk_ref[pl.ds(hd, n, stride=XA_HEADS), :].astype(BF16)
            mv = mv_ref[pl.ds(hd, n, stride=XA_HEADS), :].astype(BF16)
        else:
            mk = mk_ref[:, sl].astype(BF16)
            mv = mv_ref[:, sl].astype(BF16)
        s = lax.dot_general(q[:, sl], mk, _NT, preferred_element_type=F32) * (XA_DH ** -0.5)
        p = jnp.exp(s - jnp.max(s, axis=-1, keepdims=True))
        p = p / jnp.sum(p, axis=-1, keepdims=True)
        o_ref[:, sl] = jnp.dot(p.astype(BF16), mv, preferred_element_type=F32).astype(o_ref.dtype)


def _xattn(xq, mk, mv, head_rows):
    nb, lq, _ = xq.shape
    tq = min(TOKEN_TILE, lq)
    q_spec = pl.BlockSpec((None, tq, XA_W), lambda b, i: (b, i, 0))
    m_spec = pl.BlockSpec((None,) + mk.shape[1:], lambda b, i: (b, 0, 0))
    return pl.pallas_call(
        functools.partial(_xattn_kernel, head_rows=head_rows),
        grid=(nb, lq // tq),
        in_specs=[q_spec, m_spec, m_spec],
        out_specs=q_spec,
        out_shape=jax.ShapeDtypeStruct((nb, lq, XA_W), BF16),
        compiler_params=_cparams(("arbitrary", "arbitrary")),
        name="xattn",
    )(xq, mk, mv)


def _merge_kernel(x_ref, g_ref, yp_ref, yd_ref, ym_ref, wgate_ref, wbp_ref, wbd_ref, wbm_ref, wo_ref, o_ref, h_scr):
    h_scr[...] = _rms_bf16(x_ref[...], g_ref[...])
    merged = None
    for i, (y_ref, wb_ref) in enumerate(((yp_ref, wbp_ref), (yd_ref, wbd_ref), (ym_ref, wbm_ref))):
        z = jnp.dot(h_scr[...], wgate_ref[:, i * D_MODEL:(i + 1) * D_MODEL], preferred_element_type=F32)
        br = jnp.dot(y_ref[...].astype(BF16), wb_ref[...], preferred_element_type=F32)
        term = _sigmoid(z) * br
        merged = term if merged is None else merged + term
    o_ref[...] = x_ref[...] + jnp.dot(merged.astype(BF16), wo_ref[...], preferred_element_type=F32)


def _merge(x, gain, yp, yd, ym, wgate, wbp, wbd, wbm, wo):
    t = x.shape[0]
    tm = min(TOKEN_TILE, t)

    def row(wd):
        return pl.BlockSpec((tm, wd), lambda i: (i, 0))

    return pl.pallas_call(
        _merge_kernel,
        grid=(t // tm,),
        in_specs=[row(D_MODEL), _resident((1, D_MODEL)), row(POOL_W), row(DA_V_W), row(XA_W),
                  _resident((D_MODEL, N_BRANCH * D_MODEL)), _resident((POOL_W, D_MODEL)),
                  _resident((DA_V_W, D_MODEL)), _resident((XA_W, D_MODEL)), _resident((D_MODEL, D_MODEL))],
        out_specs=row(D_MODEL),
        out_shape=jax.ShapeDtypeStruct((t, D_MODEL), F32),
        scratch_shapes=[pltpu.VMEM((tm, D_MODEL), BF16)],
        compiler_params=_cparams(("arbitrary",)),
        name="merge",
    )(x, gain, yp, yd, ym, wgate, wbp, wbd, wbm, wo)


def kernel(x_prompt, x_sample, mem_prompt, cache_k, cache_v, state_pool, cache_mem_k, cache_mem_v, page_table, ffn1_norm, ffn1_w_gate, ffn1_w_up, ffn1_w_down, mix_norm, w_in, pool_w, pool_scale, lambda_q1, lambda_k1, lambda_q2, lambda_k2, subln_gain, mem_norm, w_mem_kv, w_br_pool, w_br_diff, w_br_mem, w_out, ffn2_norm, ffn2_w_gate, ffn2_w_up, ffn2_w_down, final_norm):
    depth = ffn1_norm.shape[0]
    assert depth == 1, "kernel is written for the single-layer configuration"
    bp, sp, _ = x_prompt.shape
    bs, ts, _ = x_sample.shape
    n_pool, page, _, _ = cache_k.shape[1:]
    past_len = page_table.shape[1] * page
    mem_len = mem_prompt.shape[1]
    li = 0
    lam_init = 0.8 - 0.6 * math.exp(-0.3 * li)
    o4 = POOL_W + 2 * DA_QK_W + DA_V_W + XA_W

    def vec(g):
        return g.reshape(1, -1).astype(F32)

    w_proj = w_in[li][:, :o4].astype(BF16)
    w_gate = w_in[li][:, o4:].astype(BF16)
    ffn1 = (vec(ffn1_norm[li]), ffn1_w_gate[li].astype(BF16), ffn1_w_up[li].astype(BF16), ffn1_w_down[li].astype(BF16))
    ffn2 = (vec(ffn2_norm[li]), ffn2_w_gate[li].astype(BF16), ffn2_w_up[li].astype(BF16), ffn2_w_down[li].astype(BF16))
    merge_w = (w_gate, w_br_pool[li].astype(BF16), w_br_diff[li].astype(BF16), w_br_mem[li].astype(BF16),
               w_out[li].astype(BF16))
    pw = pool_w[li].astype(BF16)
    ps = vec(pool_scale[li])
    sub_g = vec(subln_gain[li])
    lam_par = jnp.zeros((8, HEAD_W), F32).at[:4, :DA_DH].set(
        jnp.stack([lambda_q1[li], lambda_k1[li], lambda_q2[li], lambda_k2[li]]).astype(F32))
    fin = vec(final_norm)

    xp = x_prompt.reshape(bp * sp, D_MODEL)
    xp1 = _ffn(xp, *ffn1)
    tabs_p = _rope_tables(jnp.arange(sp))
    u_p, q_p, k_p, v_p, xq_p = _inproj(xp1, vec(mix_norm[li]), w_proj, tabs_p, BF16)
    u_p3 = u_p.reshape(bp, sp, POOL_W)
    yp_p = _pool(u_p3, pw, ps, 0, sp, 0).reshape(bp * sp, POOL_W)
    yd_p = _attn_prompt(lam_par, q_p.reshape(bp, sp, DA_QK_W), k_p.reshape(bp, sp, DA_QK_W),
                        v_p.reshape(bp, sp, DA_V_W), sub_g, lam_init).reshape(bp * sp, DA_V_W)
    mk_p, mv_p = _memkv(mem_prompt.reshape(bp * mem_len, D_MODEL), vec(mem_norm[li]), w_mem_kv[li].astype(BF16))
    ym_p = _xattn(xq_p.reshape(bp, sp, XA_W), mk_p.reshape(bp, mem_len, XA_W),
                  mv_p.reshape(bp, mem_len, XA_W), False).reshape(bp * sp, XA_W)
    xp2 = _merge(xp1, vec(mix_norm[li]), yp_p, yd_p, ym_p, *merge_w)
    y_prompt = _ffn(xp2, *ffn2, final_gain=fin).reshape(bp, sp, D_MODEL)

    xs = x_sample.reshape(bs * ts, D_MODEL)
    xs1 = _ffn(xs, *ffn1)
    tabs_s = _rope_tables(jnp.tile(past_len + jnp.arange(ts), bs))
    u_s, q_s, k_s, v_s, xq_s = _inproj(xs1, vec(mix_norm[li]), w_proj, tabs_s, F32)
    u_s3 = u_s.reshape(bs, ts, POOL_W)
    lx_s = 2 * POOL_PAD
    u_ext = jnp.concatenate([jnp.zeros((bs, POOL_PAD - POOL_HIST, POOL_W), F32), state_pool[li].astype(F32), u_s3,
                             jnp.zeros((bs, lx_s - POOL_PAD - ts, POOL_W), F32)], axis=1)
    yp_s = _pool(u_ext, pw, ps, POOL_PAD, POOL_PAD, past_len - POOL_PAD)[:, :ts].reshape(bs * ts, POOL_W)
    q_s3 = q_s.reshape(bs, ts, DA_QK_W)
    rows = 2 * DA_HEADS * ts
    r_blk = jnp.arange(rows)[:, None] // ts
    l_blk = jnp.arange(DA_QK_W)[None, :] // DA_DH
    wt = jnp.where((r_blk == l_blk)[None], jnp.tile(q_s3, (1, 2 * DA_HEADS, 1)), 0.0).astype(BF16)
    pad_new = ((0, 0), (0, page - ts), (0, 0))
    k_s3 = k_s.reshape(bs, ts, DA_QK_W)
    v_s3 = v_s.reshape(bs, ts, DA_V_W)
    yd_s = _attn_sample(page_table, lam_par, wt, jnp.pad(k_s3, pad_new), jnp.pad(v_s3, pad_new), sub_g,
                        cache_k[li].reshape(n_pool, page * DA_HEADS, HEAD_W),
                        cache_v[li].reshape(n_pool, page * DA_HEADS, DA_DV),
                        lam_init).reshape(bs * ts, DA_V_W)
    xq_pad = jnp.pad(xq_s.reshape(bs, ts, XA_W), ((0, 0), (0, 8 - ts), (0, 0)))
    ym_s = _xattn(xq_pad, cache_mem_k[li].reshape(bs, mem_len * XA_HEADS, XA_DH),
                  cache_mem_v[li].reshape(bs, mem_len * XA_HEADS, XA_DH), True)[:, :ts].reshape(bs * ts, XA_W)
    xs2 = _merge(xs1, vec(mix_norm[li]), yp_s, yd_s, ym_s, *merge_w)
    y_sample = _ffn(xs2, *ffn2, final_gain=fin).reshape(bs, ts, D_MODEL)

    k_prompt = k_p.reshape(1, bp, sp, DA_HEADS, 2 * DA_DH)
    v_prompt = v_p.reshape(1, bp, sp, DA_HEADS, DA_DV)
    pool_prompt = u_p3[:, sp - POOL_HIST:][None]
    mem_k_prompt = mk_p.reshape(1, bp, mem_len, XA_HEADS, XA_DH)
    mem_v_prompt = mv_p.reshape(1, bp, mem_len, XA_HEADS, XA_DH)
    k_sample = k_s.reshape(1, bs, ts, DA_HEADS, 2 * DA_DH)
    v_sample = v_s.reshape(1, bs, ts, DA_HEADS, DA_DV)
    pool_sample = u_ext[:, POOL_PAD + ts - POOL_HIST:POOL_PAD + ts][None]
    return (y_prompt, y_sample, k_prompt, v_prompt, pool_prompt, mem_k_prompt, mem_v_prompt, k_sample, v_sample,
            pool_sample)
```

```python
import functools
import math

import jax
import jax.numpy as jnp
from jax import lax
from jax.experimental import pallas as pl
from jax.experimental.pallas import tpu as pltpu

F32 = jnp.float32
BF16 = jnp.bfloat16

D_MODEL = 1024
DA_HEADS = 8
DA_DH = 64
DA_DV = 2 * DA_DH
HEAD_W = 2 * DA_DH
ROT_DIM = DA_DH // 4
ROPE_THETA = 500000.0
POOL_WINDOWS = (2, 4, 8, 16)
POOL_GROUP = 128
POOL_W = POOL_GROUP * len(POOL_WINDOWS)
POOL_HIST = max(POOL_WINDOWS) - 1
XA_HEADS = 4
XA_DH = 128
XA_W = XA_HEADS * XA_DH
N_BRANCH = 3
D_FF = 2816
DA_QK_W = DA_HEADS * 2 * DA_DH
DA_V_W = DA_HEADS * DA_DV
NORM_EPS = 1e-6
SUBLN_EPS = 1e-5
NEG_INF = -1e30

VMEM_LIMIT_BYTES = 56 * 1024 * 1024
TOKEN_TILE = 512
FF_CHUNK = 256
ATTN_BQ = 256
ATTN_HEADS_PER_STEP = 4
PAGES_PER_STEP = 8
PAGE_SLOTS = 3
NEW_KEY_ROWS = 16
SAMPLE_SEQS_PER_STEP = 8
POOL_PAD = 16
LOG2_E = math.log2(math.e)
Q_SCALE = DA_DH ** -0.5 * LOG2_E


def _cparams(sem):
    return pltpu.CompilerParams(dimension_semantics=sem, vmem_limit_bytes=VMEM_LIMIT_BYTES)


def _resident(shape):
    nd = len(shape)
    return pl.BlockSpec(shape, lambda *_: (0,) * nd, pipeline_mode=pl.Buffered(1))


def _rms_bf16(x, g):
    ms = jnp.mean(x * x, axis=-1, keepdims=True)
    return (x * lax.rsqrt(ms + NORM_EPS) * g).astype(BF16)


def _sigmoid(z):
    return 1.0 / (1.0 + jnp.exp(-z))


def _ffn_kernel(x_ref, g_ref, wg_ref, wu_ref, wd_ref, *rest, final):
    if final:
        fn_ref, o_ref, h_scr, a_scr = rest
    else:
        o_ref, h_scr, a_scr = rest
    h_scr[...] = _rms_bf16(x_ref[...], g_ref[...])
    for c in range(D_FF // FF_CHUNK):
        sl = slice(c * FF_CHUNK, (c + 1) * FF_CHUNK)
        hb = h_scr[...]
        g = jnp.dot(hb, wg_ref[:, sl], preferred_element_type=F32)
        u = jnp.dot(hb, wu_ref[:, sl], preferred_element_type=F32)
        a_scr[:, sl] = (g * _sigmoid(g) * u).astype(BF16)
    y = jnp.dot(a_scr[...], wd_ref[...], preferred_element_type=F32)
    x2 = x_ref[...] + 0.5 * y
    if final:
        ms = jnp.mean(x2 * x2, axis=-1, keepdims=True)
        x2 = x2 * lax.rsqrt(ms + NORM_EPS) * fn_ref[...]
    o_ref[...] = x2


def _ffn(x, gain, wg, wu, wd, final_gain=None):
    t = x.shape[0]
    tm = min(TOKEN_TILE, t)
    final = final_gain is not None
    row = pl.BlockSpec((tm, D_MODEL), lambda i: (i, 0))
    in_specs = [row, _resident((1, D_MODEL)), _resident((D_MODEL, D_FF)), _resident((D_MODEL, D_FF)),
                _resident((D_FF, D_MODEL))]
    args = [x, gain, wg, wu, wd]
    if final:
        in_specs.append(_resident((1, D_MODEL)))
        args.append(final_gain)
    return pl.pallas_call(
        functools.partial(_ffn_kernel, final=final),
        grid=(t // tm,),
        in_specs=in_specs,
        out_specs=row,
        out_shape=jax.ShapeDtypeStruct((t, D_MODEL), F32),
        scratch_shapes=[pltpu.VMEM((tm, D_MODEL), BF16), pltpu.VMEM((tm, D_FF), BF16)],
        compiler_params=_cparams(("arbitrary",)),
        name="ffn_final" if final else "ffn",
    )(*args)


def _rope(z, c, a, b):
    return z * c + pltpu.roll(z, HEAD_W - ROT_DIM // 2, 1) * a + pltpu.roll(z, ROT_DIM // 2, 1) * b


def _inproj_kernel(x_ref, g_ref, w_ref, c_ref, a_ref, b_ref, u_ref, q_ref, k_ref, v_ref, xq_ref, h_scr):
    h_scr[...] = _rms_bf16(x_ref[...], g_ref[...])
    o0 = POOL_W
    o1 = o0 + DA_QK_W
    o2 = o1 + DA_QK_W
    o3 = o2 + DA_V_W
    o4 = o3 + XA_W
    u_ref[...] = jnp.dot(h_scr[...], w_ref[:, :o0], preferred_element_type=F32)
    c, a, b = c_ref[...], a_ref[...], b_ref[...]
    zq = jnp.dot(h_scr[...], w_ref[:, o0:o1], preferred_element_type=F32)
    for hd in range(DA_HEADS):
        sl = slice(hd * HEAD_W, (hd + 1) * HEAD_W)
        q_ref[:, sl] = (_rope(zq[:, sl], c, a, b) * Q_SCALE).astype(q_ref.dtype)
    zk = jnp.dot(h_scr[...], w_ref[:, o1:o2], preferred_element_type=F32)
    for hd in range(DA_HEADS):
        sl = slice(hd * HEAD_W, (hd + 1) * HEAD_W)
        k_ref[:, sl] = _rope(zk[:, sl], c, a, b)
    v_ref[...] = jnp.dot(h_scr[...], w_ref[:, o2:o3], preferred_element_type=F32)
    xq_ref[...] = jnp.dot(h_scr[...], w_ref[:, o3:o4], preferred_element_type=F32)


def _inproj(x, gain, w, tabs, q_dtype):
    t = x.shape[0]
    tm = min(TOKEN_TILE, t)
    n_tab = tabs[0].shape[0] // tm
    ncol = w.shape[1]

    def row(wd):
        return pl.BlockSpec((tm, wd), lambda i: (i, 0))

    tab = pl.BlockSpec((tm, HEAD_W), lambda i: (i % n_tab, 0))
    return pl.pallas_call(
        _inproj_kernel,
        grid=(t // tm,),
        in_specs=[row(D_MODEL), _resident((1, D_MODEL)), _resident((D_MODEL, ncol)), tab, tab, tab],
        out_specs=[row(POOL_W), row(DA_QK_W), row(DA_QK_W), row(DA_V_W), row(XA_W)],
        out_shape=[jax.ShapeDtypeStruct((t, POOL_W), F32), jax.ShapeDtypeStruct((t, DA_QK_W), q_dtype),
                   jax.ShapeDtypeStruct((t, DA_QK_W), F32), jax.ShapeDtypeStruct((t, DA_V_W), F32),
                   jax.ShapeDtypeStruct((t, XA_W), F32)],
        scratch_shapes=[pltpu.VMEM((tm, D_MODEL), BF16)],
        compiler_params=_cparams(("arbitrary",)),
        name="inproj",
    )(x, gain, w, *tabs)


def _rope_tables(pos):
    half = ROT_DIM // 2
    inv = ROPE_THETA ** (-jnp.arange(half, dtype=F32) * 2.0 / ROT_DIM)
    ang = pos.astype(F32)[:, None] * inv[None, :]
    cos, sin = jnp.cos(ang), jnp.sin(ang)
    n = pos.shape[0]
    one = jnp.ones((n, DA_DH - ROT_DIM), F32)
    zero = jnp.zeros((n, DA_DH - ROT_DIM), F32)
    zh = jnp.zeros((n, half), F32)
    c = jnp.concatenate([cos, cos, one], axis=1)
    a = jnp.concatenate([-sin, zh, zero], axis=1)
    b = jnp.concatenate([zh, sin, zero], axis=1)
    return tuple(jnp.concatenate([m, m], axis=1) for m in (c, a, b))


def _pool_kernel(u_ref, pw_ref, ps_ref, y_ref, scr, *, lx, out_off, out_len, first_pos):
    ns = u_ref.shape[0]
    scr[:, 0:POOL_PAD, :] = jnp.zeros((ns, POOL_PAD, POOL_W), F32)
    scr[:, POOL_PAD:POOL_PAD + lx, :] = u_ref[...]
    base = POOL_PAD + out_off
    pos = first_pos + out_off + lax.broadcasted_iota(jnp.int32, (out_len, 1), 0)
    for g, w in enumerate(POOL_WINDOWS):
        cs = slice(g * POOL_GROUP, (g + 1) * POOL_GROUP)
        acc = scr[:, base:base + out_len, cs]
        for j in range(1, w):
            acc = acc + scr[:, base - j:base - j + out_len, cs]
        cnt = jnp.clip(pos + 1, 1, w).astype(F32)
        d = (acc / cnt - scr[:, base:base + out_len, cs]).reshape(ns * out_len, POOL_GROUP)
        y = jnp.dot(d.astype(BF16), pw_ref[g], preferred_element_type=F32) * ps_ref[:, cs]
        y_ref[:, :, cs] = y.reshape(ns, out_len, POOL_GROUP).astype(y_ref.dtype)


def _pool(u_ext, pool_w, pool_scale, out_off, out_len, first_pos, seqs_per_step):
    nb, lx, _ = u_ext.shape
    ns = seqs_per_step
    return pl.pallas_call(
        functools.partial(_pool_kernel, lx=lx, out_off=out_off, out_len=out_len, first_pos=first_pos),
        grid=(nb // ns,),
        in_specs=[pl.BlockSpec((ns, lx, POOL_W), lambda i: (i, 0, 0)),
                  _resident((len(POOL_WINDOWS), POOL_GROUP, POOL_GROUP)), _resident((1, POOL_W))],
        out_specs=pl.BlockSpec((ns, out_len, POOL_W), lambda i: (i, 0, 0)),
        out_shape=jax.ShapeDtypeStruct((nb, out_len, POOL_W), BF16),
        scratch_shapes=[pltpu.VMEM((ns, POOL_PAD + lx, POOL_W), F32)],
        compiler_params=_cparams(("arbitrary",)),
        name="pool",
    )(u_ext, pool_w, pool_scale)


def _diff_lambda(lp_ref, lam_init):
    lp = lp_ref[...]
    s1 = jnp.sum(lp[0:1] * lp[1:2], axis=-1, keepdims=True)
    s2 = jnp.sum(lp[2:3] * lp[3:4], axis=-1, keepdims=True)
    return jnp.exp(s1) - jnp.exp(s2) + lam_init


def _subln(o, gain, lam_init):
    ms = jnp.mean(o * o, axis=-1, keepdims=True)
    return o * lax.rsqrt(ms + SUBLN_EPS) * gain * (1.0 - lam_init)


_NT = (((1,), (1,)), ((), ()))


def _attn_prompt_kernel(lp_ref, q_ref, k_ref, v_ref, gain_ref, o_ref, kb, vt, *, lam_init):
    qi = pl.program_id(2)
    bq = ATTN_BQ
    nh = q_ref.shape[1] // HEAD_W
    nblk = vt.shape[0]

    @pl.when(qi == 0)
    def _():
        kb[...] = k_ref[...].astype(BF16)
        for hd in range(nh):
            for jb in range(nblk):
                vblk = v_ref[jb * bq:(jb + 1) * bq, hd * DA_DV:(hd + 1) * DA_DV]
                vt[jb, hd * DA_DV:(hd + 1) * DA_DV, :] = vblk.T.astype(BF16)

    dim = lax.broadcasted_iota(jnp.int32, (HEAD_W, bq), 0)
    qts = []
    for hd in range(nh):
        qt = q_ref[:, hd * HEAD_W:(hd + 1) * HEAD_W].astype(F32).T
        qts.append(jnp.concatenate([jnp.where(dim < DA_DH, qt, 0.0), jnp.where(dim >= DA_DH, qt, 0.0)],
                                   axis=1).astype(BF16))
    key = lax.broadcasted_iota(jnp.int32, (bq, 2 * bq), 0)
    qry = lax.broadcasted_iota(jnp.int32, (bq, 2 * bq), 1)
    causal = key <= jnp.where(qry >= bq, qry - bq, qry)

    def block(j, carry, diag):
        start = pl.multiple_of(j * bq, bq)
        def score(hd):
            return jnp.dot(kb[pl.ds(start, bq), hd * HEAD_W:(hd + 1) * HEAD_W], qts[hd],
                           preferred_element_type=F32)

        def softmax(hd, s):
            m, l, _ = carry[hd]
            if diag:
                s = jnp.where(causal, s, NEG_INF)
            m_new = jnp.maximum(m, jnp.max(s, axis=0, keepdims=True))
            alpha = jnp.exp2(m - m_new)
            p = jnp.exp2(s - m_new)
            return m_new, alpha * l + jnp.sum(p, axis=0, keepdims=True), alpha, p.astype(BF16)

        def value(hd, st):
            m_new, l, alpha, p = st
            pv = jnp.dot(vt[j, hd * DA_DV:(hd + 1) * DA_DV, :], p, preferred_element_type=F32)
            return m_new, l, alpha * carry[hd][2] + pv

        scores = [score(hd) for hd in range(nh)]
        stats = [softmax(hd, scores[hd]) for hd in range(nh)]
        return tuple(value(hd, stats[hd]) for hd in range(nh))

    init = tuple((jnp.full((1, 2 * bq), NEG_INF, F32), jnp.zeros((1, 2 * bq), F32),
                  jnp.zeros((DA_DV, 2 * bq), F32)) for _ in range(nh))
    carry = lax.fori_loop(0, qi, lambda j, c: block(j, c, False), init)
    carry = block(qi, carry, True)
    lam = _diff_lambda(lp_ref, lam_init)
    for hd in range(nh):
        _, l, acc = carry[hd]
        o = acc / l
        out_t = o[:, :bq] - lam * o[:, bq:]
        ms = jnp.mean(out_t * out_t, axis=0, keepdims=True)
        y = (out_t * lax.rsqrt(ms + SUBLN_EPS)).T * gain_ref[...] * (1.0 - lam_init)
        o_ref[:, hd * DA_DV:(hd + 1) * DA_DV] = y.astype(o_ref.dtype)


def _attn_prompt(lam_par, q, k, v, gain, lam_init):
    nb, s, _ = q.shape
    bq = ATTN_BQ
    nh = ATTN_HEADS_PER_STEP
    kv_spec = pl.BlockSpec((None, s, nh * HEAD_W), lambda b, h, i: (b, 0, h))
    q_spec = pl.BlockSpec((None, bq, nh * HEAD_W), lambda b, h, i: (b, i, h))
    return pl.pallas_call(
        functools.partial(_attn_prompt_kernel, lam_init=lam_init),
        grid=(nb, DA_HEADS // nh, s // bq),
        in_specs=[_resident((8, HEAD_W)), q_spec, kv_spec, kv_spec, _resident((1, DA_DV))],
        out_specs=q_spec,
        out_shape=jax.ShapeDtypeStruct((nb, s, DA_V_W), BF16),
        scratch_shapes=[pltpu.VMEM((s, nh * HEAD_W), BF16), pltpu.VMEM((s // bq, nh * DA_DV, bq), BF16)],
        compiler_params=_cparams(("arbitrary", "arbitrary", "arbitrary")),
        name="attn_prompt",
    )(lam_par, q, k, v, gain)


def _attn_sample_kernel(pt_ref, lp_ref, wt_ref, kn_ref, vn_ref, gain_ref, ck_hbm, cv_hbm, o_ref,
                        kbuf, vbuf, sem, m_s, l_s, acc_s, *, lam_init, chunks_per_seq, page):
    c = pl.program_id(0)
    nchunk = pl.num_programs(0)
    ci = c % chunks_per_seq
    npg = PAGES_PER_STEP
    rows = acc_s.shape[0]

    def copies(cc, slot):
        out = []
        for p in range(npg):
            pg = pt_ref[cc * npg + p]
            dst = pl.ds(p * page * DA_HEADS, page * DA_HEADS)
            out.append(pltpu.make_async_copy(ck_hbm.at[pg], kbuf.at[slot, dst, :], sem.at[0, slot]))
            out.append(pltpu.make_async_copy(cv_hbm.at[pg], vbuf.at[slot, dst, :], sem.at[1, slot]))
        return out

    ahead = PAGE_SLOTS - 1

    @pl.when(c == 0)
    def _():
        for cc in range(ahead):
            for d in copies(cc, cc):
                d.start()

    for d in copies(jnp.minimum(c + ahead, nchunk - 1), (c + ahead) % PAGE_SLOTS):
        d.start()

    slot = c % PAGE_SLOTS
    for d in copies(c, slot):
        d.wait()

    @pl.when(ci == 0)
    def _():
        m_s[...] = jnp.full(m_s.shape, NEG_INF, F32)
        l_s[...] = jnp.zeros(l_s.shape, F32)
        acc_s[...] = jnp.zeros(acc_s.shape, F32)

    wt = wt_ref[...]

    def update(kblk, vblk, mask):
        s = lax.dot_general(wt, kblk, _NT, preferred_element_type=F32)
        if mask is not None:
            s = jnp.where(mask, s, NEG_INF)
        m_old = m_s[...]
        m_new = jnp.maximum(m_old, jnp.max(s, axis=-1, keepdims=True))
        alpha = jnp.exp2(m_old - m_new)
        p = jnp.exp2(s - m_new[:, :1])
        l_s[...] = alpha * l_s[...] + jnp.sum(p, axis=-1, keepdims=True)
        acc_s[...] = alpha[:, :1] * acc_s[...] + jnp.dot(p.astype(BF16), vblk, preferred_element_type=F32)
        m_s[...] = m_new

    keys = npg * page

    def heads_to_lanes(buf):
        ref = buf.at[slot]
        cols = [ref[pl.ds(hd, keys, stride=DA_HEADS), :] for hd in range(DA_HEADS)]
        return jnp.concatenate(cols, axis=1).astype(BF16)

    update(heads_to_lanes(kbuf), heads_to_lanes(vbuf), None)

    @pl.when(ci == chunks_per_seq - 1)
    def _():
        nt = o_ref.shape[0]
        nk = kn_ref.shape[0]
        r = lax.broadcasted_iota(jnp.int32, (rows, nk), 0)
        col = lax.broadcasted_iota(jnp.int32, (rows, nk), 1)
        update(kn_ref[...].astype(BF16), vn_ref[...].astype(BF16), col <= r % nt)
        o = acc_s[...] / l_s[...][:, :1]
        rh = lax.broadcasted_iota(jnp.int32, o.shape, 0) // (2 * nt)
        lh = lax.broadcasted_iota(jnp.int32, o.shape, 1) // DA_DV
        o = jnp.where(rh == lh, o, 0.0)
        red = jnp.sum(o.reshape(DA_HEADS, 2 * nt, DA_V_W), axis=0)
        lam = _diff_lambda(lp_ref, lam_init)
        out = red[:nt] - lam * red[nt:]
        gain = gain_ref[...]
        for hd in range(DA_HEADS):
            sl = slice(hd * DA_DV, (hd + 1) * DA_DV)
            o_ref[:, sl] = _subln(out[:, sl], gain, lam_init)

    @pl.when(c == nchunk - 1)
    def _():
        for k in range(1, ahead + 1):
            for d in copies(c, (c + k) % PAGE_SLOTS):
                d.wait()


def _attn_sample(page_table, lam_par, wt, k_new, v_new, gain, cache_k, cache_v, lam_init):
    nb, n_pages = page_table.shape
    page = cache_k.shape[1] // DA_HEADS
    kw = DA_HEADS * cache_k.shape[2]
    rows = wt.shape[1]
    nt = rows // (2 * DA_HEADS)
    nk = k_new.shape[1]
    cps = n_pages // PAGES_PER_STEP
    keys = PAGES_PER_STEP * page
    grid_spec = pltpu.PrefetchScalarGridSpec(
        num_scalar_prefetch=1,
        grid=(nb * cps,),
        in_specs=[
            pl.BlockSpec((8, HEAD_W), lambda c, pt: (0, 0)),
            pl.BlockSpec((None, rows, kw), lambda c, pt: (c // cps, 0, 0)),
            pl.BlockSpec((None, nk, kw), lambda c, pt: (c // cps, 0, 0)),
            pl.BlockSpec((None, nk, kw), lambda c, pt: (c // cps, 0, 0)),
            pl.BlockSpec((1, DA_DV), lambda c, pt: (0, 0)),
            pl.BlockSpec(memory_space=pl.ANY),
            pl.BlockSpec(memory_space=pl.ANY),
        ],
        out_specs=pl.BlockSpec((None, nt, kw), lambda c, pt: (c // cps, 0, 0)),
        scratch_shapes=[
            pltpu.VMEM((PAGE_SLOTS, keys * DA_HEADS, HEAD_W), F32),
            pltpu.VMEM((PAGE_SLOTS, keys * DA_HEADS, DA_DV), F32),
            pltpu.SemaphoreType.DMA((2, PAGE_SLOTS)),
            pltpu.VMEM((rows, HEAD_W), F32), pltpu.VMEM((rows, HEAD_W), F32), pltpu.VMEM((rows, kw), F32),
        ],
    )
    return pl.pallas_call(
        functools.partial(_attn_sample_kernel, lam_init=lam_init, chunks_per_seq=cps, page=page),
        grid_spec=grid_spec,
        out_shape=jax.ShapeDtypeStruct((nb, nt, kw), F32),
        compiler_params=_cparams(("arbitrary",)),
        name="attn_sample",
    )(page_table.reshape(-1), lam_par, wt, k_new, v_new, gain, cache_k, cache_v)


def _memkv_kernel(x_ref, g_ref, w_ref, k_ref, v_ref):
    h = _rms_bf16(x_ref[...], g_ref[...])
    k_ref[...] = jnp.dot(h, w_ref[:, :XA_W], preferred_element_type=F32)
    v_ref[...] = jnp.dot(h, w_ref[:, XA_W:], preferred_element_type=F32)


def _memkv(x, gain, w):
    t = x.shape[0]
    tm = min(TOKEN_TILE, t)

    def row(wd):
        return pl.BlockSpec((tm, wd), lambda i: (i, 0))

    return pl.pallas_call(
        _memkv_kernel,
        grid=(t // tm,),
        in_specs=[row(D_MODEL), _resident((1, D_MODEL)), _resident((D_MODEL, 2 * XA_W))],
        out_specs=[row(XA_W), row(XA_W)],
        out_shape=[jax.ShapeDtypeStruct((t, XA_W), F32)] * 2,
        compiler_params=_cparams(("arbitrary",)),
        name="memkv",
    )(x, gain, w)


def _xattn_kernel(q_ref, mk_ref, mv_ref, o_ref, *, head_rows):
    for b in range(q_ref.shape[0]):
        q = q_ref[b].astype(BF16)
        for hd in range(XA_HEADS):
            sl = slice(hd * XA_DH, (hd + 1) * XA_DH)
            if head_rows:
                n = mk_ref.shape[1] // XA_HEADS
                mk = mk_ref[b, pl.ds(hd, n, stride=XA_HEADS), :].astype(BF16)
                mv = mv_ref[b, pl.ds(hd, n, stride=XA_HEADS), :].astype(BF16)
            else:
                mk = mk_ref[b, :, sl].astype(BF16)
                mv = mv_ref[b, :, sl].astype(BF16)
            s = lax.dot_general(q[:, sl], mk, _NT, preferred_element_type=F32) * (XA_DH ** -0.5)
            p = jnp.exp(s - jnp.max(s, axis=-1, keepdims=True))
            p = p / jnp.sum(p, axis=-1, keepdims=True)
            o_ref[b, :, sl] = jnp.dot(p.astype(BF16), mv, preferred_element_type=F32).astype(o_ref.dtype)


def _xattn(xq, mk, mv, head_rows, seqs_per_step):
    nb, lq, _ = xq.shape
    tq = min(TOKEN_TILE, lq)
    ns = seqs_per_step
    q_spec = pl.BlockSpec((ns, tq, XA_W), lambda b, i: (b, i, 0))
    m_spec = pl.BlockSpec((ns,) + mk.shape[1:], lambda b, i: (b, 0, 0))
    return pl.pallas_call(
        functools.partial(_xattn_kernel, head_rows=head_rows),
        grid=(nb // ns, lq // tq),
        in_specs=[q_spec, m_spec, m_spec],
        out_specs=q_spec,
        out_shape=jax.ShapeDtypeStruct((nb, lq, XA_W), BF16),
        compiler_params=_cparams(("arbitrary", "arbitrary")),
        name="xattn",
    )(xq, mk, mv)


def _merge_kernel(x_ref, g_ref, yp_ref, yd_ref, ym_ref, wgate_ref, wbp_ref, wbd_ref, wbm_ref, wo_ref, o_ref, h_scr):
    h_scr[...] = _rms_bf16(x_ref[...], g_ref[...])
    merged = None
    for i, (y_ref, wb_ref) in enumerate(((yp_ref, wbp_ref), (yd_ref, wbd_ref), (ym_ref, wbm_ref))):
        z = jnp.dot(h_scr[...], wgate_ref[:, i * D_MODEL:(i + 1) * D_MODEL], preferred_element_type=F32)
        br = jnp.dot(y_ref[...].astype(BF16), wb_ref[...], preferred_element_type=F32)
        term = _sigmoid(z) * br
        merged = term if merged is None else merged + term
    o_ref[...] = x_ref[...] + jnp.dot(merged.astype(BF16), wo_ref[...], preferred_element_type=F32)


def _merge(x, gain, yp, yd, ym, wgate, wbp, wbd, wbm, wo):
    t = x.shape[0]
    tm = min(TOKEN_TILE, t)

    def row(wd):
        return pl.BlockSpec((tm, wd), lambda i: (i, 0))

    return pl.pallas_call(
        _merge_kernel,
        grid=(t // tm,),
        in_specs=[row(D_MODEL), _resident((1, D_MODEL)), row(POOL_W), row(DA_V_W), row(XA_W),
                  _resident((D_MODEL, N_BRANCH * D_MODEL)), _resident((POOL_W, D_MODEL)),
                  _resident((DA_V_W, D_MODEL)), _resident((XA_W, D_MODEL)), _resident((D_MODEL, D_MODEL))],
        out_specs=row(D_MODEL),
        out_shape=jax.ShapeDtypeStruct((t, D_MODEL), F32),
        scratch_shapes=[pltpu.VMEM((tm, D_MODEL), BF16)],
        compiler_params=_cparams(("arbitrary",)),
        name="merge",
    )(x, gain, yp, yd, ym, wgate, wbp, wbd, wbm, wo)


def kernel(x_prompt, x_sample, mem_prompt, cache_k, cache_v, state_pool, cache_mem_k, cache_mem_v, page_table, ffn1_norm, ffn1_w_gate, ffn1_w_up, ffn1_w_down, mix_norm, w_in, pool_w, pool_scale, lambda_q1, lambda_k1, lambda_q2, lambda_k2, subln_gain, mem_norm, w_mem_kv, w_br_pool, w_br_diff, w_br_mem, w_out, ffn2_norm, ffn2_w_gate, ffn2_w_up, ffn2_w_down, final_norm):
    depth = ffn1_norm.shape[0]
    assert depth == 1, "kernel is written for the single-layer configuration"
    bp, sp, _ = x_prompt.shape
    bs, ts, _ = x_sample.shape
    n_pool, page, _, _ = cache_k.shape[1:]
    past_len = page_table.shape[1] * page
    mem_len = mem_prompt.shape[1]
    li = 0
    lam_init = 0.8 - 0.6 * math.exp(-0.3 * li)
    o4 = POOL_W + 2 * DA_QK_W + DA_V_W + XA_W

    def vec(g):
        return g.reshape(1, -1).astype(F32)

    w_proj = w_in[li][:, :o4].astype(BF16)
    w_gate = w_in[li][:, o4:].astype(BF16)
    ffn1 = (vec(ffn1_norm[li]), ffn1_w_gate[li].astype(BF16), ffn1_w_up[li].astype(BF16), ffn1_w_down[li].astype(BF16))
    ffn2 = (vec(ffn2_norm[li]), ffn2_w_gate[li].astype(BF16), ffn2_w_up[li].astype(BF16), ffn2_w_down[li].astype(BF16))
    merge_w = (w_gate, w_br_pool[li].astype(BF16), w_br_diff[li].astype(BF16), w_br_mem[li].astype(BF16),
               w_out[li].astype(BF16))
    pw = pool_w[li].astype(BF16)
    ps = vec(pool_scale[li])
    sub_g = vec(subln_gain[li])
    lam_par = jnp.zeros((8, HEAD_W), F32).at[:4, :DA_DH].set(
        jnp.stack([lambda_q1[li], lambda_k1[li], lambda_q2[li], lambda_k2[li]]).astype(F32))
    fin = vec(final_norm)

    xp = x_prompt.reshape(bp * sp, D_MODEL)
    xp1 = _ffn(xp, *ffn1)
    tabs_p = _rope_tables(jnp.arange(sp))
    u_p, q_p, k_p, v_p, xq_p = _inproj(xp1, vec(mix_norm[li]), w_proj, tabs_p, BF16)
    u_p3 = u_p.reshape(bp, sp, POOL_W)
    yp_p = _pool(u_p3, pw, ps, 0, sp, 0, 1).reshape(bp * sp, POOL_W)
    yd_p = _attn_prompt(lam_par, q_p.reshape(bp, sp, DA_QK_W), k_p.reshape(bp, sp, DA_QK_W),
                        v_p.reshape(bp, sp, DA_V_W), sub_g, lam_init).reshape(bp * sp, DA_V_W)
    mk_p, mv_p = _memkv(mem_prompt.reshape(bp * mem_len, D_MODEL), vec(mem_norm[li]), w_mem_kv[li].astype(BF16))
    ym_p = _xattn(xq_p.reshape(bp, sp, XA_W), mk_p.reshape(bp, mem_len, XA_W),
                  mv_p.reshape(bp, mem_len, XA_W), False, 1).reshape(bp * sp, XA_W)
    xp2 = _merge(xp1, vec(mix_norm[li]), yp_p, yd_p, ym_p, *merge_w)
    y_prompt = _ffn(xp2, *ffn2, final_gain=fin).reshape(bp, sp, D_MODEL)

    xs = x_sample.reshape(bs * ts, D_MODEL)
    xs1 = _ffn(xs, *ffn1)
    tabs_s = _rope_tables(jnp.tile(past_len + jnp.arange(ts), bs))
    u_s, q_s, k_s, v_s, xq_s = _inproj(xs1, vec(mix_norm[li]), w_proj, tabs_s, F32)
    u_s3 = u_s.reshape(bs, ts, POOL_W)
    lx_s = 2 * POOL_PAD
    u_ext = jnp.concatenate([jnp.zeros((bs, POOL_PAD - POOL_HIST, POOL_W), F32), state_pool[li].astype(F32), u_s3,
                             jnp.zeros((bs, lx_s - POOL_PAD - ts, POOL_W), F32)], axis=1)
    yp_s = _pool(u_ext, pw, ps, POOL_PAD, POOL_PAD, past_len - POOL_PAD,
                 SAMPLE_SEQS_PER_STEP)[:, :ts].reshape(bs * ts, POOL_W)
    q_s3 = q_s.reshape(bs, ts, DA_QK_W)
    rows = 2 * DA_HEADS * ts
    r_blk = jnp.arange(rows)[:, None] // ts
    l_blk = jnp.arange(DA_QK_W)[None, :] // DA_DH
    wt = jnp.where((r_blk == l_blk)[None], jnp.tile(q_s3, (1, 2 * DA_HEADS, 1)), 0.0).astype(BF16)
    pad_new = ((0, 0), (0, NEW_KEY_ROWS - ts), (0, 0))
    k_s3 = k_s.reshape(bs, ts, DA_QK_W)
    v_s3 = v_s.reshape(bs, ts, DA_V_W)
    yd_s = _attn_sample(page_table, lam_par, wt, jnp.pad(k_s3, pad_new), jnp.pad(v_s3, pad_new), sub_g,
                        cache_k[li].reshape(n_pool, page * DA_HEADS, HEAD_W),
                        cache_v[li].reshape(n_pool, page * DA_HEADS, DA_DV),
                        lam_init).reshape(bs * ts, DA_V_W)
    xq_pad = jnp.pad(xq_s.reshape(bs, ts, XA_W), ((0, 0), (0, 8 - ts), (0, 0)))
    ym_s = _xattn(xq_pad, cache_mem_k[li].reshape(bs, mem_len * XA_HEADS, XA_DH),
                  cache_mem_v[li].reshape(bs, mem_len * XA_HEADS, XA_DH), True,
                  SAMPLE_SEQS_PER_STEP)[:, :ts].reshape(bs * ts, XA_W)
    xs2 = _merge(xs1, vec(mix_norm[li]), yp_s, yd_s, ym_s, *merge_w)
    y_sample = _ffn(xs2, *ffn2, final_gain=fin).reshape(bs, ts, D_MODEL)

    k_prompt = k_p.reshape(1, bp, sp, DA_HEADS, 2 * DA_DH)
    v_prompt = v_p.reshape(1, bp, sp, DA_HEADS, DA_DV)
    pool_prompt = u_p3[:, sp - POOL_HIST:][None]
    mem_k_prompt = mk_p.reshape(1, bp, mem_len, XA_HEADS, XA_DH)
    mem_v_prompt = mv_p.reshape(1, bp, mem_len, XA_HEADS, XA_DH)
    k_sample = k_s.reshape(1, bs, ts, DA_HEADS, 2 * DA_DH)
    v_sample = v_s.reshape(1, bs, ts, DA_HEADS, DA_DV)
    pool_sample = u_ext[:, POOL_PAD + ts - POOL_HIST:POOL_PAD + ts][None]
    return (y_prompt, y_sample, k_prompt, v_prompt, pool_prompt, mem_k_prompt, mem_v_prompt, k_sample, v_sample,
            pool_sample)
```

```python
import functools
import math

import jax
import jax.numpy as jnp
from jax import lax
from jax.experimental import pallas as pl
from jax.experimental.pallas import tpu as pltpu

F32 = jnp.float32
BF16 = jnp.bfloat16

D_MODEL = 1024
DA_HEADS = 8
DA_DH = 64
DA_DV = 2 * DA_DH
HEAD_W = 2 * DA_DH
ROT_DIM = DA_DH // 4
ROPE_THETA = 500000.0
POOL_WINDOWS = (2, 4, 8, 16)
POOL_GROUP = 128
POOL_W = POOL_GROUP * len(POOL_WINDOWS)
POOL_HIST = max(POOL_WINDOWS) - 1
XA_HEADS = 4
XA_DH = 128
XA_W = XA_HEADS * XA_DH
N_BRANCH = 3
D_FF = 2816
DA_QK_W = DA_HEADS * 2 * DA_DH
DA_V_W = DA_HEADS * DA_DV
NORM_EPS = 1e-6
SUBLN_EPS = 1e-5
NEG_INF = -1e30

VMEM_LIMIT_BYTES = 56 * 1024 * 1024
TOKEN_TILE = 512
FF_CHUNK = 256
ATTN_BQ = 256
ATTN_HEADS_PER_STEP = 4
PAGES_PER_STEP = 8
PAGE_SLOTS = 3
NEW_KEY_ROWS = 16
SAMPLE_SEQS_PER_STEP = 8
POOL_PAD = 16
LOG2_E = math.log2(math.e)
Q_SCALE = DA_DH ** -0.5 * LOG2_E


def _cparams(sem):
    return pltpu.CompilerParams(dimension_semantics=sem, vmem_limit_bytes=VMEM_LIMIT_BYTES)


def _resident(shape):
    nd = len(shape)
    return pl.BlockSpec(shape, lambda *_: (0,) * nd, pipeline_mode=pl.Buffered(1))


def _rms_bf16(x, g):
    ms = jnp.mean(x * x, axis=-1, keepdims=True)
    return (x * lax.rsqrt(ms + NORM_EPS) * g).astype(BF16)


def _sigmoid(z):
    return 1.0 / (1.0 + jnp.exp(-z))


def _ffn_kernel(x_ref, g_ref, wg_ref, wu_ref, wd_ref, *rest, final):
    if final:
        fn_ref, o_ref, h_scr, a_scr = rest
    else:
        o_ref, h_scr, a_scr = rest
    h_scr[...] = _rms_bf16(x_ref[...], g_ref[...])
    for c in range(D_FF // FF_CHUNK):
        sl = slice(c * FF_CHUNK, (c + 1) * FF_CHUNK)
        hb = h_scr[...]
        g = jnp.dot(hb, wg_ref[:, sl], preferred_element_type=F32)
        u = jnp.dot(hb, wu_ref[:, sl], preferred_element_type=F32)
        a_scr[:, sl] = (g * _sigmoid(g) * u).astype(BF16)
    y = jnp.dot(a_scr[...], wd_ref[...], preferred_element_type=F32)
    x2 = x_ref[...] + 0.5 * y
    if final:
        ms = jnp.mean(x2 * x2, axis=-1, keepdims=True)
        x2 = x2 * lax.rsqrt(ms + NORM_EPS) * fn_ref[...]
    o_ref[...] = x2


def _ffn(x, gain, wg, wu, wd, final_gain=None):
    t = x.shape[0]
    tm = min(TOKEN_TILE, t)
    final = final_gain is not None
    row = pl.BlockSpec((tm, D_MODEL), lambda i: (i, 0))
    in_specs = [row, _resident((1, D_MODEL)), _resident((D_MODEL, D_FF)), _resident((D_MODEL, D_FF)),
                _resident((D_FF, D_MODEL))]
    args = [x, gain, wg, wu, wd]
    if final:
        in_specs.append(_resident((1, D_MODEL)))
        args.append(final_gain)
    return pl.pallas_call(
        functools.partial(_ffn_kernel, final=final),
        grid=(t // tm,),
        in_specs=in_specs,
        out_specs=row,
        out_shape=jax.ShapeDtypeStruct((t, D_MODEL), F32),
        scratch_shapes=[pltpu.VMEM((tm, D_MODEL), BF16), pltpu.VMEM((tm, D_FF), BF16)],
        compiler_params=_cparams(("arbitrary",)),
        name="ffn_final" if final else "ffn",
    )(*args)


def _rope(z, c, a, b):
    return z * c + pltpu.roll(z, HEAD_W - ROT_DIM // 2, 1) * a + pltpu.roll(z, ROT_DIM // 2, 1) * b


def _inproj_kernel(x_ref, g_ref, w_ref, c_ref, a_ref, b_ref, u_ref, q_ref, k_ref, v_ref, xq_ref, h_scr):
    h_scr[...] = _rms_bf16(x_ref[...], g_ref[...])
    o0 = POOL_W
    o1 = o0 + DA_QK_W
    o2 = o1 + DA_QK_W
    o3 = o2 + DA_V_W
    o4 = o3 + XA_W
    u_ref[...] = jnp.dot(h_scr[...], w_ref[:, :o0], preferred_element_type=F32)
    c, a, b = c_ref[...], a_ref[...], b_ref[...]
    zq = jnp.dot(h_scr[...], w_ref[:, o0:o1], preferred_element_type=F32)
    for hd in range(DA_HEADS):
        sl = slice(hd * HEAD_W, (hd + 1) * HEAD_W)
        q_ref[:, sl] = (_rope(zq[:, sl], c, a, b) * Q_SCALE).astype(q_ref.dtype)
    zk = jnp.dot(h_scr[...], w_ref[:, o1:o2], preferred_element_type=F32)
    for hd in range(DA_HEADS):
        sl = slice(hd * HEAD_W, (hd + 1) * HEAD_W)
        k_ref[:, sl] = _rope(zk[:, sl], c, a, b)
    v_ref[...] = jnp.dot(h_scr[...], w_ref[:, o2:o3], preferred_element_type=F32)
    xq_ref[...] = jnp.dot(h_scr[...], w_ref[:, o3:o4], preferred_element_type=F32)


def _inproj(x, gain, w, tabs, q_dtype):
    t = x.shape[0]
    tm = min(TOKEN_TILE, t)
    n_tab = tabs[0].shape[0] // tm
    ncol = w.shape[1]

    def row(wd):
        return pl.BlockSpec((tm, wd), lambda i: (i, 0))

    tab = pl.BlockSpec((tm, HEAD_W), lambda i: (i % n_tab, 0))
    return pl.pallas_call(
        _inproj_kernel,
        grid=(t // tm,),
        in_specs=[row(D_MODEL), _resident((1, D_MODEL)), _resident((D_MODEL, ncol)), tab, tab, tab],
        out_specs=[row(POOL_W), row(DA_QK_W), row(DA_QK_W), row(DA_V_W), row(XA_W)],
        out_shape=[jax.ShapeDtypeStruct((t, POOL_W), F32), jax.ShapeDtypeStruct((t, DA_QK_W), q_dtype),
                   jax.ShapeDtypeStruct((t, DA_QK_W), F32), jax.ShapeDtypeStruct((t, DA_V_W), F32),
                   jax.ShapeDtypeStruct((t, XA_W), F32)],
        scratch_shapes=[pltpu.VMEM((tm, D_MODEL), BF16)],
        compiler_params=_cparams(("arbitrary",)),
        name="inproj",
    )(x, gain, w, *tabs)


def _rope_tables(pos):
    half = ROT_DIM // 2
    inv = ROPE_THETA ** (-jnp.arange(half, dtype=F32) * 2.0 / ROT_DIM)
    ang = pos.astype(F32)[:, None] * inv[None, :]
    cos, sin = jnp.cos(ang), jnp.sin(ang)
    n = pos.shape[0]
    one = jnp.ones((n, DA_DH - ROT_DIM), F32)
    zero = jnp.zeros((n, DA_DH - ROT_DIM), F32)
    zh = jnp.zeros((n, half), F32)
    c = jnp.concatenate([cos, cos, one], axis=1)
    a = jnp.concatenate([-sin, zh, zero], axis=1)
    b = jnp.concatenate([zh, sin, zero], axis=1)
    return tuple(jnp.concatenate([m, m], axis=1) for m in (c, a, b))


def _pool_kernel(u_ref, pw_ref, ps_ref, y_ref, scr, *, lx, out_off, out_len, first_pos):
    ns = u_ref.shape[0]
    scr[:, 0:POOL_PAD, :] = jnp.zeros((ns, POOL_PAD, POOL_W), F32)
    scr[:, POOL_PAD:POOL_PAD + lx, :] = u_ref[...]
    base = POOL_PAD + out_off
    pos = first_pos + out_off + lax.broadcasted_iota(jnp.int32, (out_len, 1), 0)
    for g, w in enumerate(POOL_WINDOWS):
        cs = slice(g * POOL_GROUP, (g + 1) * POOL_GROUP)
        acc = scr[:, base:base + out_len, cs]
        for j in range(1, w):
            acc = acc + scr[:, base - j:base - j + out_len, cs]
        cnt = jnp.clip(pos + 1, 1, w).astype(F32)
        d = (acc / cnt - scr[:, base:base + out_len, cs]).reshape(ns * out_len, POOL_GROUP)
        y = jnp.dot(d.astype(BF16), pw_ref[g], preferred_element_type=F32) * ps_ref[:, cs]
        y_ref[:, :, cs] = y.reshape(ns, out_len, POOL_GROUP).astype(y_ref.dtype)


def _pool(u_ext, pool_w, pool_scale, out_off, out_len, first_pos, seqs_per_step):
    nb, lx, _ = u_ext.shape
    ns = seqs_per_step
    return pl.pallas_call(
        functools.partial(_pool_kernel, lx=lx, out_off=out_off, out_len=out_len, first_pos=first_pos),
        grid=(nb // ns,),
        in_specs=[pl.BlockSpec((ns, lx, POOL_W), lambda i: (i, 0, 0)),
                  _resident((len(POOL_WINDOWS), POOL_GROUP, POOL_GROUP)), _resident((1, POOL_W))],
        out_specs=pl.BlockSpec((ns, out_len, POOL_W), lambda i: (i, 0, 0)),
        out_shape=jax.ShapeDtypeStruct((nb, out_len, POOL_W), BF16),
        scratch_shapes=[pltpu.VMEM((ns, POOL_PAD + lx, POOL_W), F32)],
        compiler_params=_cparams(("arbitrary",)),
        name="pool",
    )(u_ext, pool_w, pool_scale)


def _diff_lambda(lp_ref, lam_init):
    lp = lp_ref[...]
    s1 = jnp.sum(lp[0:1] * lp[1:2], axis=-1, keepdims=True)
    s2 = jnp.sum(lp[2:3] * lp[3:4], axis=-1, keepdims=True)
    return jnp.exp(s1) - jnp.exp(s2) + lam_init


def _subln(o, gain, lam_init):
    ms = jnp.mean(o * o, axis=-1, keepdims=True)
    return o * lax.rsqrt(ms + SUBLN_EPS) * gain * (1.0 - lam_init)


_NT = (((1,), (1,)), ((), ()))


def _prompt_attention_tile(qi, lp_ref, q_ref, k_ref, v_ref, gain_ref, o_ref, kb, vt, lam_init):
    bq = ATTN_BQ
    nh = q_ref.shape[1] // HEAD_W
    nblk = vt.shape[0]

    @pl.when(qi == 0)
    def _():
        kb[...] = k_ref[...].astype(BF16)
        for hd in range(nh):
            for jb in range(nblk):
                vblk = v_ref[jb * bq:(jb + 1) * bq, hd * DA_DV:(hd + 1) * DA_DV]
                vt[jb, hd * DA_DV:(hd + 1) * DA_DV, :] = vblk.T.astype(BF16)

    dim = lax.broadcasted_iota(jnp.int32, (HEAD_W, bq), 0)
    qts = []
    for hd in range(nh):
        qt = q_ref[:, hd * HEAD_W:(hd + 1) * HEAD_W].astype(F32).T
        qts.append(jnp.concatenate([jnp.where(dim < DA_DH, qt, 0.0), jnp.where(dim >= DA_DH, qt, 0.0)],
                                   axis=1).astype(BF16))
    key = lax.broadcasted_iota(jnp.int32, (bq, 2 * bq), 0)
    qry = lax.broadcasted_iota(jnp.int32, (bq, 2 * bq), 1)
    causal = key <= jnp.where(qry >= bq, qry - bq, qry)

    def block(j, carry, diag):
        start = pl.multiple_of(j * bq, bq)
        def score(hd):
            return jnp.dot(kb[pl.ds(start, bq), hd * HEAD_W:(hd + 1) * HEAD_W], qts[hd],
                           preferred_element_type=F32)

        def softmax(hd, s):
            m, l, _ = carry[hd]
            if diag:
                s = jnp.where(causal, s, NEG_INF)
            m_new = jnp.maximum(m, jnp.max(s, axis=0, keepdims=True))
            alpha = jnp.exp2(m - m_new)
            p = jnp.exp2(s - m_new)
            return m_new, alpha * l + jnp.sum(p, axis=0, keepdims=True), alpha, p.astype(BF16)

        def value(hd, st):
            m_new, l, alpha, p = st
            pv = jnp.dot(vt[j, hd * DA_DV:(hd + 1) * DA_DV, :], p, preferred_element_type=F32)
            return m_new, l, alpha * carry[hd][2] + pv

        scores = [score(hd) for hd in range(nh)]
        stats = [softmax(hd, scores[hd]) for hd in range(nh)]
        return tuple(value(hd, stats[hd]) for hd in range(nh))

    init = tuple((jnp.full((1, 2 * bq), NEG_INF, F32), jnp.zeros((1, 2 * bq), F32),
                  jnp.zeros((DA_DV, 2 * bq), F32)) for _ in range(nh))
    carry = lax.fori_loop(0, qi, lambda j, c: block(j, c, False), init)
    carry = block(qi, carry, True)
    lam = _diff_lambda(lp_ref, lam_init)
    for hd in range(nh):
        _, l, acc = carry[hd]
        o = acc / l
        out_t = o[:, :bq] - lam * o[:, bq:]
        ms = jnp.mean(out_t * out_t, axis=0, keepdims=True)
        y = (out_t * lax.rsqrt(ms + SUBLN_EPS)).T * gain_ref[...] * (1.0 - lam_init)
        o_ref[:, hd * DA_DV:(hd + 1) * DA_DV] = y.astype(o_ref.dtype)


def _sample_attention_chunk(c, nchunk, pt_ref, lp_ref, wt_ref, kn_ref, vn_ref, gain_ref, ck_hbm, cv_hbm, o_ref,
                            kbuf, vbuf, sem, m_s, l_s, acc_s, lam_init, chunks_per_seq, page):
    ci = c % chunks_per_seq
    npg = PAGES_PER_STEP
    rows = acc_s.shape[0]

    def copies(cc, slot):
        out = []
        for p in range(npg):
            pg = pt_ref[cc * npg + p]
            dst = pl.ds(p * page * DA_HEADS, page * DA_HEADS)
            out.append(pltpu.make_async_copy(ck_hbm.at[pg], kbuf.at[slot, dst, :], sem.at[0, slot]))
            out.append(pltpu.make_async_copy(cv_hbm.at[pg], vbuf.at[slot, dst, :], sem.at[1, slot]))
        return out

    ahead = PAGE_SLOTS - 1

    @pl.when(c == 0)
    def _():
        for cc in range(ahead):
            for d in copies(cc, cc):
                d.start()

    for d in copies(jnp.minimum(c + ahead, nchunk - 1), (c + ahead) % PAGE_SLOTS):
        d.start()

    slot = c % PAGE_SLOTS
    for d in copies(c, slot):
        d.wait()

    @pl.when(ci == 0)
    def _():
        m_s[...] = jnp.full(m_s.shape, NEG_INF, F32)
        l_s[...] = jnp.zeros(l_s.shape, F32)
        acc_s[...] = jnp.zeros(acc_s.shape, F32)

    wt = wt_ref[...]

    def update(kblk, vblk, mask):
        s = lax.dot_general(wt, kblk, _NT, preferred_element_type=F32)
        if mask is not None:
            s = jnp.where(mask, s, NEG_INF)
        m_old = m_s[...]
        m_new = jnp.maximum(m_old, jnp.max(s, axis=-1, keepdims=True))
        alpha = jnp.exp2(m_old - m_new)
        p = jnp.exp2(s - m_new[:, :1])
        l_s[...] = alpha * l_s[...] + jnp.sum(p, axis=-1, keepdims=True)
        acc_s[...] = alpha[:, :1] * acc_s[...] + jnp.dot(p.astype(BF16), vblk, preferred_element_type=F32)
        m_s[...] = m_new

    keys = npg * page

    def heads_to_lanes(buf):
        ref = buf.at[slot]
        cols = [ref[pl.ds(hd, keys, stride=DA_HEADS), :] for hd in range(DA_HEADS)]
        return jnp.concatenate(cols, axis=1).astype(BF16)

    update(heads_to_lanes(kbuf), heads_to_lanes(vbuf), None)

    @pl.when(ci == chunks_per_seq - 1)
    def _():
        nt = o_ref.shape[0]
        nk = kn_ref.shape[0]
        r = lax.broadcasted_iota(jnp.int32, (rows, nk), 0)
        col = lax.broadcasted_iota(jnp.int32, (rows, nk), 1)
        update(kn_ref[...].astype(BF16), vn_ref[...].astype(BF16), col <= r % nt)
        o = acc_s[...] / l_s[...][:, :1]
        rh = lax.broadcasted_iota(jnp.int32, o.shape, 0) // (2 * nt)
        lh = lax.broadcasted_iota(jnp.int32, o.shape, 1) // DA_DV
        o = jnp.where(rh == lh, o, 0.0)
        red = jnp.sum(o.reshape(DA_HEADS, 2 * nt, DA_V_W), axis=0)
        lam = _diff_lambda(lp_ref, lam_init)
        out = red[:nt] - lam * red[nt:]
        gain = gain_ref[...]
        for hd in range(DA_HEADS):
            sl = slice(hd * DA_DV, (hd + 1) * DA_DV)
            o_ref[:, sl] = _subln(out[:, sl], gain, lam_init)

    @pl.when(c == nchunk - 1)
    def _():
        for k in range(1, ahead + 1):
            for d in copies(c, (c + k) % PAGE_SLOTS):
                d.wait()


def _attn_kernel(pt_ref, lp_ref, gain_ref, q_ref, k_ref, v_ref, wt_ref, kn_ref, vn_ref, ck_hbm, cv_hbm,
                 op_ref, os_ref, kb, vt, kbuf, vbuf, sem, m_s, l_s, acc_s,
                 *, lam_init, chunks_per_seq, chunks_per_step, page):
    step = ((pl.program_id(0) * pl.num_programs(1) + pl.program_id(1)) * pl.num_programs(2)
            + pl.program_id(2))
    nchunk = pl.num_programs(0) * pl.num_programs(1) * pl.num_programs(2) * chunks_per_step

    def chunk(i, carry):
        _sample_attention_chunk(step * chunks_per_step + i, nchunk, pt_ref, lp_ref, wt_ref, kn_ref, vn_ref,
                                gain_ref, ck_hbm, cv_hbm, os_ref, kbuf, vbuf, sem, m_s, l_s, acc_s,
                                lam_init, chunks_per_seq, page)
        return carry

    lax.fori_loop(0, chunks_per_step, chunk, 0)
    _prompt_attention_tile(pl.program_id(2), lp_ref, q_ref, k_ref, v_ref, gain_ref, op_ref, kb, vt, lam_init)


def _attention(page_table, lam_par, gain, q, k, v, wt, k_new, v_new, cache_k, cache_v, lam_init):
    nb, s, _ = q.shape
    bq = ATTN_BQ
    nh = ATTN_HEADS_PER_STEP
    grid = (nb, DA_HEADS // nh, s // bq)
    n_seq, n_pages = page_table.shape
    page = cache_k.shape[1] // DA_HEADS
    kw = DA_HEADS * cache_k.shape[2]
    rows = wt.shape[1]
    nt = rows // (2 * DA_HEADS)
    nk = k_new.shape[1]
    cps = n_pages // PAGES_PER_STEP
    n_steps = grid[0] * grid[1] * grid[2]
    chunks_per_step = n_seq * cps // n_steps
    assert chunks_per_step * n_steps == n_seq * cps and cps % chunks_per_step == 0
    steps_per_seq = cps // chunks_per_step
    keys = PAGES_PER_STEP * page

    def seq_of(b, h, i):
        return ((b * grid[1] + h) * grid[2] + i) // steps_per_seq

    kv_spec = pl.BlockSpec((None, s, nh * HEAD_W), lambda b, h, i, pt: (b, 0, h))
    q_spec = pl.BlockSpec((None, bq, nh * HEAD_W), lambda b, h, i, pt: (b, i, h))
    grid_spec = pltpu.PrefetchScalarGridSpec(
        num_scalar_prefetch=1,
        grid=grid,
        in_specs=[
            pl.BlockSpec((8, HEAD_W), lambda b, h, i, pt: (0, 0)),
            pl.BlockSpec((1, DA_DV), lambda b, h, i, pt: (0, 0)),
            q_spec, kv_spec, kv_spec,
            pl.BlockSpec((None, rows, kw), lambda b, h, i, pt: (seq_of(b, h, i), 0, 0)),
            pl.BlockSpec((None, nk, kw), lambda b, h, i, pt: (seq_of(b, h, i), 0, 0)),
            pl.BlockSpec((None, nk, kw), lambda b, h, i, pt: (seq_of(b, h, i), 0, 0)),
            pl.BlockSpec(memory_space=pl.ANY),
            pl.BlockSpec(memory_space=pl.ANY),
        ],
        out_specs=[q_spec, pl.BlockSpec((None, nt, kw), lambda b, h, i, pt: (seq_of(b, h, i), 0, 0))],
        scratch_shapes=[
            pltpu.VMEM((s, nh * HEAD_W), BF16), pltpu.VMEM((s // bq, nh * DA_DV, bq), BF16),
            pltpu.VMEM((PAGE_SLOTS, keys * DA_HEADS, HEAD_W), F32),
            pltpu.VMEM((PAGE_SLOTS, keys * DA_HEADS, DA_DV), F32),
            pltpu.SemaphoreType.DMA((2, PAGE_SLOTS)),
            pltpu.VMEM((rows, HEAD_W), F32), pltpu.VMEM((rows, HEAD_W), F32), pltpu.VMEM((rows, kw), F32),
        ],
    )
    return pl.pallas_call(
        functools.partial(_attn_kernel, lam_init=lam_init, chunks_per_seq=cps, chunks_per_step=chunks_per_step,
                          page=page),
        grid_spec=grid_spec,
        out_shape=[jax.ShapeDtypeStruct((nb, s, DA_V_W), BF16), jax.ShapeDtypeStruct((n_seq, nt, kw), F32)],
        compiler_params=_cparams(("arbitrary", "arbitrary", "arbitrary")),
        name="attn",
    )(page_table.reshape(-1), lam_par, gain, q, k, v, wt, k_new, v_new, cache_k, cache_v)


def _memkv_kernel(x_ref, g_ref, w_ref, k_ref, v_ref):
    h = _rms_bf16(x_ref[...], g_ref[...])
    k_ref[...] = jnp.dot(h, w_ref[:, :XA_W], preferred_element_type=F32)
    v_ref[...] = jnp.dot(h, w_ref[:, XA_W:], preferred_element_type=F32)


def _memkv(x, gain, w):
    t = x.shape[0]
    tm = min(TOKEN_TILE, t)

    def row(wd):
        return pl.BlockSpec((tm, wd), lambda i: (i, 0))

    return pl.pallas_call(
        _memkv_kernel,
        grid=(t // tm,),
        in_specs=[row(D_MODEL), _resident((1, D_MODEL)), _resident((D_MODEL, 2 * XA_W))],
        out_specs=[row(XA_W), row(XA_W)],
        out_shape=[jax.ShapeDtypeStruct((t, XA_W), F32)] * 2,
        compiler_params=_cparams(("arbitrary",)),
        name="memkv",
    )(x, gain, w)


def _xattn_kernel(q_ref, mk_ref, mv_ref, o_ref, *, head_rows):
    def mem(ref, b, hd):
        if head_rows:
            n = ref.shape[1] // XA_HEADS
            return ref[b, pl.ds(hd, n, stride=XA_HEADS), :].astype(BF16)
        return ref[b, :, hd * XA_DH:(hd + 1) * XA_DH].astype(BF16)

    pairs = [(b, hd) for b in range(q_ref.shape[0]) for hd in range(XA_HEADS)]
    qs = [q_ref[b].astype(BF16) for b in range(q_ref.shape[0])]
    scores = [lax.dot_general(qs[b][:, hd * XA_DH:(hd + 1) * XA_DH], mem(mk_ref, b, hd), _NT,
                              preferred_element_type=F32) * (XA_DH ** -0.5) for b, hd in pairs]
    probs = []
    for s in scores:
        p = jnp.exp(s - jnp.max(s, axis=-1, keepdims=True))
        probs.append((p / jnp.sum(p, axis=-1, keepdims=True)).astype(BF16))
    for (b, hd), p in zip(pairs, probs):
        o_ref[b, :, hd * XA_DH:(hd + 1) * XA_DH] = jnp.dot(
            p, mem(mv_ref, b, hd), preferred_element_type=F32).astype(o_ref.dtype)


def _xattn(xq, mk, mv, head_rows, seqs_per_step):
    nb, lq, _ = xq.shape
    tq = min(TOKEN_TILE, lq)
    ns = seqs_per_step
    q_spec = pl.BlockSpec((ns, tq, XA_W), lambda b, i: (b, i, 0))
    m_spec = pl.BlockSpec((ns,) + mk.shape[1:], lambda b, i: (b, 0, 0))
    return pl.pallas_call(
        functools.partial(_xattn_kernel, head_rows=head_rows),
        grid=(nb // ns, lq // tq),
        in_specs=[q_spec, m_spec, m_spec],
        out_specs=q_spec,
        out_shape=jax.ShapeDtypeStruct((nb, lq, XA_W), BF16),
        compiler_params=_cparams(("arbitrary", "arbitrary")),
        name="xattn",
    )(xq, mk, mv)


def _merge_kernel(x_ref, g_ref, yp_ref, yd_ref, ym_ref, wgate_ref, wbp_ref, wbd_ref, wbm_ref, wo_ref, o_ref, h_scr):
    h_scr[...] = _rms_bf16(x_ref[...], g_ref[...])
    merged = None
    for i, (y_ref, wb_ref) in enumerate(((yp_ref, wbp_ref), (yd_ref, wbd_ref), (ym_ref, wbm_ref))):
        g0 = wgate_ref.shape[1] - (N_BRANCH - i) * D_MODEL
        z = jnp.dot(h_scr[...], wgate_ref[:, g0:g0 + D_MODEL], preferred_element_type=F32)
        br = jnp.dot(y_ref[...].astype(BF16), wb_ref[...], preferred_element_type=F32)
        term = _sigmoid(z) * br
        merged = term if merged is None else merged + term
    o_ref[...] = x_ref[...] + jnp.dot(merged.astype(BF16), wo_ref[...], preferred_element_type=F32)


def _merge(x, gain, yp, yd, ym, wgate, wbp, wbd, wbm, wo):
    t = x.shape[0]
    tm = min(TOKEN_TILE, t)

    def row(wd):
        return pl.BlockSpec((tm, wd), lambda i: (i, 0))

    return pl.pallas_call(
        _merge_kernel,
        grid=(t // tm,),
        in_specs=[row(D_MODEL), _resident((1, D_MODEL)), row(POOL_W), row(DA_V_W), row(XA_W),
                  _resident(wgate.shape), _resident((POOL_W, D_MODEL)),
                  _resident((DA_V_W, D_MODEL)), _resident((XA_W, D_MODEL)), _resident((D_MODEL, D_MODEL))],
        out_specs=row(D_MODEL),
        out_shape=jax.ShapeDtypeStruct((t, D_MODEL), F32),
        scratch_shapes=[pltpu.VMEM((tm, D_MODEL), BF16)],
        compiler_params=_cparams(("arbitrary",)),
        name="merge",
    )(x, gain, yp, yd, ym, wgate, wbp, wbd, wbm, wo)


def kernel(x_prompt, x_sample, mem_prompt, cache_k, cache_v, state_pool, cache_mem_k, cache_mem_v, page_table, ffn1_norm, ffn1_w_gate, ffn1_w_up, ffn1_w_down, mix_norm, w_in, pool_w, pool_scale, lambda_q1, lambda_k1, lambda_q2, lambda_k2, subln_gain, mem_norm, w_mem_kv, w_br_pool, w_br_diff, w_br_mem, w_out, ffn2_norm, ffn2_w_gate, ffn2_w_up, ffn2_w_down, final_norm):
    depth = ffn1_norm.shape[0]
    assert depth == 1, "kernel is written for the single-layer configuration"
    bp, sp, _ = x_prompt.shape
    bs, ts, _ = x_sample.shape
    n_pool, page, _, _ = cache_k.shape[1:]
    past_len = page_table.shape[1] * page
    mem_len = mem_prompt.shape[1]
    li = 0
    lam_init = 0.8 - 0.6 * math.exp(-0.3 * li)

    def vec(g):
        return g.reshape(1, -1).astype(F32)

    w_proj = w_gate = w_in[li].astype(BF16)
    ffn1 = (vec(ffn1_norm[li]), ffn1_w_gate[li].astype(BF16), ffn1_w_up[li].astype(BF16), ffn1_w_down[li].astype(BF16))
    ffn2 = (vec(ffn2_norm[li]), ffn2_w_gate[li].astype(BF16), ffn2_w_up[li].astype(BF16), ffn2_w_down[li].astype(BF16))
    merge_w = (w_gate, w_br_pool[li].astype(BF16), w_br_diff[li].astype(BF16), w_br_mem[li].astype(BF16),
               w_out[li].astype(BF16))
    pw = pool_w[li].astype(BF16)
    ps = vec(pool_scale[li])
    sub_g = vec(subln_gain[li])
    lam_par = jnp.zeros((8, HEAD_W), F32).at[:4, :DA_DH].set(
        jnp.stack([lambda_q1[li], lambda_k1[li], lambda_q2[li], lambda_k2[li]]).astype(F32))
    fin = vec(final_norm)

    xp = x_prompt.reshape(bp * sp, D_MODEL)
    xp1 = _ffn(xp, *ffn1)
    tabs_p = _rope_tables(jnp.arange(sp))
    u_p, q_p, k_p, v_p, xq_p = _inproj(xp1, vec(mix_norm[li]), w_proj, tabs_p, BF16)
    u_p3 = u_p.reshape(bp, sp, POOL_W)
    yp_p = _pool(u_p3, pw, ps, 0, sp, 0, 1).reshape(bp * sp, POOL_W)
    mk_p, mv_p = _memkv(mem_prompt.reshape(bp * mem_len, D_MODEL), vec(mem_norm[li]), w_mem_kv[li].astype(BF16))
    ym_p = _xattn(xq_p.reshape(bp, sp, XA_W), mk_p.reshape(bp, mem_len, XA_W),
                  mv_p.reshape(bp, mem_len, XA_W), False, 1).reshape(bp * sp, XA_W)

    xs = x_sample.reshape(bs * ts, D_MODEL)
    xs1 = _ffn(xs, *ffn1)
    tabs_s = _rope_tables(jnp.tile(past_len + jnp.arange(ts), bs))
    u_s, q_s, k_s, v_s, xq_s = _inproj(xs1, vec(mix_norm[li]), w_proj, tabs_s, F32)
    u_s3 = u_s.reshape(bs, ts, POOL_W)
    lx_s = 2 * POOL_PAD
    u_ext = jnp.concatenate([jnp.zeros((bs, POOL_PAD - POOL_HIST, POOL_W), F32), state_pool[li].astype(F32), u_s3,
                             jnp.zeros((bs, lx_s - POOL_PAD - ts, POOL_W), F32)], axis=1)
    yp_s = _pool(u_ext, pw, ps, POOL_PAD, POOL_PAD, past_len - POOL_PAD,
                 SAMPLE_SEQS_PER_STEP)[:, :ts].reshape(bs * ts, POOL_W)
    q_s3 = q_s.reshape(bs, ts, DA_QK_W)
    rows = 2 * DA_HEADS * ts
    r_blk = jnp.arange(rows)[:, None] // ts
    l_blk = jnp.arange(DA_QK_W)[None, :] // DA_DH
    wt = jnp.where((r_blk == l_blk)[None], jnp.tile(q_s3, (1, 2 * DA_HEADS, 1)), 0.0).astype(BF16)
    pad_new = ((0, 0), (0, NEW_KEY_ROWS - ts), (0, 0))
    k_s3 = k_s.reshape(bs, ts, DA_QK_W)
    v_s3 = v_s.reshape(bs, ts, DA_V_W)
    yd_p, yd_s = _attention(page_table, lam_par, sub_g, q_p.reshape(bp, sp, DA_QK_W), k_p.reshape(bp, sp, DA_QK_W),
                            v_p.reshape(bp, sp, DA_V_W), wt, jnp.pad(k_s3, pad_new), jnp.pad(v_s3, pad_new),
                            cache_k[li].reshape(n_pool, page * DA_HEADS, HEAD_W),
                            cache_v[li].reshape(n_pool, page * DA_HEADS, DA_DV), lam_init)
    yd_p = yd_p.reshape(bp * sp, DA_V_W)
    yd_s = yd_s.reshape(bs * ts, DA_V_W)
    xp2 = _merge(xp1, vec(mix_norm[li]), yp_p, yd_p, ym_p, *merge_w)
    y_prompt = _ffn(xp2, *ffn2, final_gain=fin).reshape(bp, sp, D_MODEL)
    xq_pad = jnp.pad(xq_s.reshape(bs, ts, XA_W), ((0, 0), (0, 8 - ts), (0, 0)))
    ym_s = _xattn(xq_pad, cache_mem_k[li].reshape(bs, mem_len * XA_HEADS, XA_DH),
                  cache_mem_v[li].reshape(bs, mem_len * XA_HEADS, XA_DH), True,
                  SAMPLE_SEQS_PER_STEP)[:, :ts].reshape(bs * ts, XA_W)
    xs2 = _merge(xs1, vec(mix_norm[li]), yp_s, yd_s, ym_s, *merge_w)
    y_sample = _ffn(xs2, *ffn2, final_gain=fin).reshape(bs, ts, D_MODEL)

    k_prompt = k_p.reshape(1, bp, sp, DA_HEADS, 2 * DA_DH)
    v_prompt = v_p.reshape(1, bp, sp, DA_HEADS, DA_DV)
    pool_prompt = u_p3[:, sp - POOL_HIST:][None]
    mem_k_prompt = mk_p.reshape(1, bp, mem_len, XA_HEADS, XA_DH)
    mem_v_prompt = mv_p.reshape(1, bp, mem_len, XA_HEADS, XA_DH)
    k_sample = k_s.reshape(1, bs, ts, DA_HEADS, 2 * DA_DH)
    v_sample = v_s.reshape(1, bs, ts, DA_HEADS, DA_DV)
    pool_sample = u_ext[:, POOL_PAD + ts - POOL_HIST:POOL_PAD + ts][None]
    return (y_prompt, y_sample, k_prompt, v_prompt, pool_prompt, mem_k_prompt, mem_v_prompt, k_sample, v_sample,
            pool_sample)
```

```python
import functools
import math

import jax
import jax.numpy as jnp
from jax import lax
from jax.experimental import pallas as pl
from jax.experimental.pallas import tpu as pltpu

F32 = jnp.float32
BF16 = jnp.bfloat16

D_MODEL = 1024
DA_HEADS = 8
DA_DH = 64
DA_DV = 2 * DA_DH
HEAD_W = 2 * DA_DH
ROT_DIM = DA_DH // 4
ROPE_THETA = 500000.0
POOL_WINDOWS = (2, 4, 8, 16)
POOL_GROUP = 128
POOL_W = POOL_GROUP * len(POOL_WINDOWS)
POOL_HIST = max(POOL_WINDOWS) - 1
XA_HEADS = 4
XA_DH = 128
XA_W = XA_HEADS * XA_DH
N_BRANCH = 3
D_FF = 2816
DA_QK_W = DA_HEADS * 2 * DA_DH
DA_V_W = DA_HEADS * DA_DV
NORM_EPS = 1e-6
SUBLN_EPS = 1e-5
NEG_INF = -1e30

VMEM_LIMIT_BYTES = 56 * 1024 * 1024
TOKEN_TILE = 512
FF_CHUNK = 256
ATTN_BQ = 256
ATTN_HEADS_PER_STEP = 4
PAGES_PER_STEP = 8
PAGE_SLOTS = 3
NEW_KEY_ROWS = 16
SAMPLE_SEQS_PER_STEP = 8
POOL_PAD = 16
LOG2_E = math.log2(math.e)
Q_SCALE = DA_DH ** -0.5 * LOG2_E


def _cparams(sem):
    return pltpu.CompilerParams(dimension_semantics=sem, vmem_limit_bytes=VMEM_LIMIT_BYTES)


def _resident(shape):
    nd = len(shape)
    return pl.BlockSpec(shape, lambda *_: (0,) * nd, pipeline_mode=pl.Buffered(1))


def _rms_bf16(x, g):
    ms = jnp.mean(x * x, axis=-1, keepdims=True)
    return (x * lax.rsqrt(ms + NORM_EPS) * g).astype(BF16)


def _sigmoid(z):
    return 1.0 / (1.0 + jnp.exp(-z))


def _ffn_kernel(x_ref, g_ref, wg_ref, wu_ref, wd_ref, *rest, final):
    if final:
        fn_ref, o_ref, h_scr, a_scr = rest
    else:
        o_ref, h_scr, a_scr = rest
    h_scr[...] = _rms_bf16(x_ref[...], g_ref[...])
    for c in range(D_FF // FF_CHUNK):
        sl = slice(c * FF_CHUNK, (c + 1) * FF_CHUNK)
        hb = h_scr[...]
        g = jnp.dot(hb, wg_ref[:, sl], preferred_element_type=F32)
        u = jnp.dot(hb, wu_ref[:, sl], preferred_element_type=F32)
        a_scr[:, sl] = (g * _sigmoid(g) * u).astype(BF16)
    y = jnp.dot(a_scr[...], wd_ref[...], preferred_element_type=F32)
    x2 = x_ref[...] + 0.5 * y
    if final:
        ms = jnp.mean(x2 * x2, axis=-1, keepdims=True)
        x2 = x2 * lax.rsqrt(ms + NORM_EPS) * fn_ref[...]
    o_ref[...] = x2


def _ffn(x, gain, wg, wu, wd, final_gain=None):
    t = x.shape[0]
    tm = min(TOKEN_TILE, t)
    final = final_gain is not None
    row = pl.BlockSpec((tm, D_MODEL), lambda i: (i, 0))
    in_specs = [row, _resident((1, D_MODEL)), _resident((D_MODEL, D_FF)), _resident((D_MODEL, D_FF)),
                _resident((D_FF, D_MODEL))]
    args = [x, gain, wg, wu, wd]
    if final:
        in_specs.append(_resident((1, D_MODEL)))
        args.append(final_gain)
    return pl.pallas_call(
        functools.partial(_ffn_kernel, final=final),
        grid=(t // tm,),
        in_specs=in_specs,
        out_specs=row,
        out_shape=jax.ShapeDtypeStruct((t, D_MODEL), F32),
        scratch_shapes=[pltpu.VMEM((tm, D_MODEL), BF16), pltpu.VMEM((tm, D_FF), BF16)],
        compiler_params=_cparams(("arbitrary",)),
        name="ffn_final" if final else "ffn",
    )(*args)


def _rope(z, c, a, b):
    return z * c + pltpu.roll(z, HEAD_W - ROT_DIM // 2, 1) * a + pltpu.roll(z, ROT_DIM // 2, 1) * b


def _inproj_kernel(x_ref, g_ref, w_ref, c_ref, a_ref, b_ref, u_ref, q_ref, k_ref, v_ref, xq_ref, h_scr):
    h_scr[...] = _rms_bf16(x_ref[...], g_ref[...])
    o0 = POOL_W
    o1 = o0 + DA_QK_W
    o2 = o1 + DA_QK_W
    o3 = o2 + DA_V_W
    o4 = o3 + XA_W
    u_ref[...] = jnp.dot(h_scr[...], w_ref[:, :o0], preferred_element_type=F32)
    c, a, b = c_ref[...], a_ref[...], b_ref[...]
    zq = jnp.dot(h_scr[...], w_ref[:, o0:o1], preferred_element_type=F32)
    for hd in range(DA_HEADS):
        sl = slice(hd * HEAD_W, (hd + 1) * HEAD_W)
        q_ref[:, sl] = (_rope(zq[:, sl], c, a, b) * Q_SCALE).astype(q_ref.dtype)
    zk = jnp.dot(h_scr[...], w_ref[:, o1:o2], preferred_element_type=F32)
    for hd in range(DA_HEADS):
        sl = slice(hd * HEAD_W, (hd + 1) * HEAD_W)
        k_ref[:, sl] = _rope(zk[:, sl], c, a, b)
    v_ref[...] = jnp.dot(h_scr[...], w_ref[:, o2:o3], preferred_element_type=F32)
    xq_ref[...] = jnp.dot(h_scr[...], w_ref[:, o3:o4], preferred_element_type=F32)


def _inproj(x, gain, w, tabs, q_dtype):
    t = x.shape[0]
    tm = min(TOKEN_TILE, t)
    n_tab = tabs[0].shape[0] // tm
    ncol = w.shape[1]

    def row(wd):
        return pl.BlockSpec((tm, wd), lambda i: (i, 0))

    tab = pl.BlockSpec((tm, HEAD_W), lambda i: (i % n_tab, 0))
    return pl.pallas_call(
        _inproj_kernel,
        grid=(t // tm,),
        in_specs=[row(D_MODEL), _resident((1, D_MODEL)), _resident((D_MODEL, ncol)), tab, tab, tab],
        out_specs=[row(POOL_W), row(DA_QK_W), row(DA_QK_W), row(DA_V_W), row(XA_W)],
        out_shape=[jax.ShapeDtypeStruct((t, POOL_W), F32), jax.ShapeDtypeStruct((t, DA_QK_W), q_dtype),
                   jax.ShapeDtypeStruct((t, DA_QK_W), F32), jax.ShapeDtypeStruct((t, DA_V_W), F32),
                   jax.ShapeDtypeStruct((t, XA_W), F32)],
        scratch_shapes=[pltpu.VMEM((tm, D_MODEL), BF16)],
        compiler_params=_cparams(("arbitrary",)),
        name="inproj",
    )(x, gain, w, *tabs)


def _rope_tables(pos):
    half = ROT_DIM // 2
    inv = ROPE_THETA ** (-jnp.arange(half, dtype=F32) * 2.0 / ROT_DIM)
    ang = pos.astype(F32)[:, None] * inv[None, :]
    cos, sin = jnp.cos(ang), jnp.sin(ang)
    n = pos.shape[0]
    one = jnp.ones((n, DA_DH - ROT_DIM), F32)
    zero = jnp.zeros((n, DA_DH - ROT_DIM), F32)
    zh = jnp.zeros((n, half), F32)
    c = jnp.concatenate([cos, cos, one], axis=1)
    a = jnp.concatenate([-sin, zh, zero], axis=1)
    b = jnp.concatenate([zh, sin, zero], axis=1)
    return tuple(jnp.concatenate([m, m], axis=1) for m in (c, a, b))


def _pool_kernel(u_ref, pw_ref, ps_ref, y_ref, scr, *, lx, out_off, out_len, first_pos):
    ns = u_ref.shape[0]
    scr[:, 0:POOL_PAD, :] = jnp.zeros((ns, POOL_PAD, POOL_W), F32)
    scr[:, POOL_PAD:POOL_PAD + lx, :] = u_ref[...]
    base = POOL_PAD + out_off
    pos = first_pos + out_off + lax.broadcasted_iota(jnp.int32, (out_len, 1), 0)
    for g, w in enumerate(POOL_WINDOWS):
        cs = slice(g * POOL_GROUP, (g + 1) * POOL_GROUP)
        acc = scr[:, base:base + out_len, cs]
        for j in range(1, w):
            acc = acc + scr[:, base - j:base - j + out_len, cs]
        cnt = jnp.clip(pos + 1, 1, w).astype(F32)
        d = (acc / cnt - scr[:, base:base + out_len, cs]).reshape(ns * out_len, POOL_GROUP)
        y = jnp.dot(d.astype(BF16), pw_ref[g], preferred_element_type=F32) * ps_ref[:, cs]
        y_ref[:, :, cs] = y.reshape(ns, out_len, POOL_GROUP).astype(y_ref.dtype)


def _pool(u_ext, pool_w, pool_scale, out_off, out_len, first_pos, seqs_per_step):
    nb, lx, _ = u_ext.shape
    ns = seqs_per_step
    return pl.pallas_call(
        functools.partial(_pool_kernel, lx=lx, out_off=out_off, out_len=out_len, first_pos=first_pos),
        grid=(nb // ns,),
        in_specs=[pl.BlockSpec((ns, lx, POOL_W), lambda i: (i, 0, 0)),
                  _resident((len(POOL_WINDOWS), POOL_GROUP, POOL_GROUP)), _resident((1, POOL_W))],
        out_specs=pl.BlockSpec((ns, out_len, POOL_W), lambda i: (i, 0, 0)),
        out_shape=jax.ShapeDtypeStruct((nb, out_len, POOL_W), BF16),
        scratch_shapes=[pltpu.VMEM((ns, POOL_PAD + lx, POOL_W), F32)],
        compiler_params=_cparams(("arbitrary",)),
        name="pool",
    )(u_ext, pool_w, pool_scale)


def _diff_lambda(lp_ref, lam_init):
    lp = lp_ref[...]
    s1 = jnp.sum(lp[0:1] * lp[1:2], axis=-1, keepdims=True)
    s2 = jnp.sum(lp[2:3] * lp[3:4], axis=-1, keepdims=True)
    return jnp.exp(s1) - jnp.exp(s2) + lam_init


def _subln(o, gain, lam_init):
    ms = jnp.mean(o * o, axis=-1, keepdims=True)
    return o * lax.rsqrt(ms + SUBLN_EPS) * gain * (1.0 - lam_init)


_NT = (((1,), (1,)), ((), ()))


def _prompt_attention_tile(qi, lp_ref, q_ref, k_ref, v_ref, gain_ref, o_ref, kb, vt, lam_init):
    bq = ATTN_BQ
    nh = q_ref.shape[1] // HEAD_W
    nblk = vt.shape[0]

    @pl.when(qi == 0)
    def _():
        kb[...] = k_ref[...].astype(BF16)
        for hd in range(nh):
            for jb in range(nblk):
                vblk = v_ref[jb * bq:(jb + 1) * bq, hd * DA_DV:(hd + 1) * DA_DV]
                vt[jb, hd * DA_DV:(hd + 1) * DA_DV, :] = vblk.T.astype(BF16)

    dim = lax.broadcasted_iota(jnp.int32, (HEAD_W, bq), 0)
    qts = []
    for hd in range(nh):
        qt = q_ref[:, hd * HEAD_W:(hd + 1) * HEAD_W].astype(F32).T
        qts.append(jnp.concatenate([jnp.where(dim < DA_DH, qt, 0.0), jnp.where(dim >= DA_DH, qt, 0.0)],
                                   axis=1).astype(BF16))
    key = lax.broadcasted_iota(jnp.int32, (bq, 2 * bq), 0)
    qry = lax.broadcasted_iota(jnp.int32, (bq, 2 * bq), 1)
    causal = key <= jnp.where(qry >= bq, qry - bq, qry)

    def block(j, carry, diag):
        start = pl.multiple_of(j * bq, bq)
        def score(hd):
            return jnp.dot(kb[pl.ds(start, bq), hd * HEAD_W:(hd + 1) * HEAD_W], qts[hd],
                           preferred_element_type=F32)

        def softmax(hd, s):
            m, l, _ = carry[hd]
            if diag:
                s = jnp.where(causal, s, NEG_INF)
            m_new = jnp.maximum(m, jnp.max(s, axis=0, keepdims=True))
            alpha = jnp.exp2(m - m_new)
            p = jnp.exp2(s - m_new)
            return m_new, alpha * l + jnp.sum(p, axis=0, keepdims=True), alpha, p.astype(BF16)

        def value(hd, st):
            m_new, l, alpha, p = st
            pv = jnp.dot(vt[j, hd * DA_DV:(hd + 1) * DA_DV, :], p, preferred_element_type=F32)
            return m_new, l, alpha * carry[hd][2] + pv

        scores = [score(hd) for hd in range(nh)]
        stats = [softmax(hd, scores[hd]) for hd in range(nh)]
        return tuple(value(hd, stats[hd]) for hd in range(nh))

    init = tuple((jnp.full((1, 2 * bq), NEG_INF, F32), jnp.zeros((1, 2 * bq), F32),
                  jnp.zeros((DA_DV, 2 * bq), F32)) for _ in range(nh))
    carry = lax.fori_loop(0, qi, lambda j, c: block(j, c, False), init)
    carry = block(qi, carry, True)
    lam = _diff_lambda(lp_ref, lam_init)
    for hd in range(nh):
        _, l, acc = carry[hd]
        o = acc / l
        out_t = o[:, :bq] - lam * o[:, bq:]
        ms = jnp.mean(out_t * out_t, axis=0, keepdims=True)
        y = (out_t * lax.rsqrt(ms + SUBLN_EPS)).T * gain_ref[...] * (1.0 - lam_init)
        o_ref[:, hd * DA_DV:(hd + 1) * DA_DV] = y.astype(o_ref.dtype)


def _sample_attention_chunk(c, nchunk, pt_ref, lp_ref, wt_ref, kn_ref, vn_ref, gain_ref, ck_hbm, cv_hbm, o_ref,
                            kbuf, vbuf, sem, m_s, l_s, acc_s, lam_init, chunks_per_seq, page):
    ci = c % chunks_per_seq
    npg = PAGES_PER_STEP
    rows = acc_s.shape[0]

    def copies(cc, slot):
        out = []
        for p in range(npg):
            pg = pt_ref[cc * npg + p]
            dst = pl.ds(p * page * DA_HEADS, page * DA_HEADS)
            out.append(pltpu.make_async_copy(ck_hbm.at[pg], kbuf.at[slot, dst, :], sem.at[0, slot]))
            out.append(pltpu.make_async_copy(cv_hbm.at[pg], vbuf.at[slot, dst, :], sem.at[1, slot]))
        return out

    ahead = PAGE_SLOTS - 1

    @pl.when(c == 0)
    def _():
        for cc in range(ahead):
            for d in copies(cc, cc):
                d.start()

    for d in copies(jnp.minimum(c + ahead, nchunk - 1), (c + ahead) % PAGE_SLOTS):
        d.start()

    slot = c % PAGE_SLOTS
    for d in copies(c, slot):
        d.wait()

    @pl.when(ci == 0)
    def _():
        m_s[...] = jnp.full(m_s.shape, NEG_INF, F32)
        l_s[...] = jnp.zeros(l_s.shape, F32)
        acc_s[...] = jnp.zeros(acc_s.shape, F32)

    wt = wt_ref[...]

    def update(kblk, vblk, mask):
        s = lax.dot_general(wt, kblk, _NT, preferred_element_type=F32)
        if mask is not None:
            s = jnp.where(mask, s, NEG_INF)
        m_old = m_s[...]
        m_new = jnp.maximum(m_old, jnp.max(s, axis=-1, keepdims=True))
        alpha = jnp.exp2(m_old - m_new)
        p = jnp.exp2(s - m_new[:, :1])
        l_s[...] = alpha * l_s[...] + jnp.sum(p, axis=-1, keepdims=True)
        acc_s[...] = alpha[:, :1] * acc_s[...] + jnp.dot(p.astype(BF16), vblk, preferred_element_type=F32)
        m_s[...] = m_new

    keys = npg * page

    half = keys // 2

    def heads_to_lanes(buf, first):
        ref = buf.at[slot]
        cols = [ref[pl.ds(first * DA_HEADS + hd, half, stride=DA_HEADS), :] for hd in range(DA_HEADS)]
        return jnp.concatenate(cols, axis=1).astype(BF16)

    sa = lax.dot_general(wt, heads_to_lanes(kbuf, 0), _NT, preferred_element_type=F32)
    sb = lax.dot_general(wt, heads_to_lanes(kbuf, half), _NT, preferred_element_type=F32)
    m_old = m_s[...]
    m_a = jnp.maximum(m_old, jnp.max(sa, axis=-1, keepdims=True))
    al_a = jnp.exp2(m_old - m_a)
    pa = jnp.exp2(sa - m_a[:, :1])
    l_a = al_a * l_s[...] + jnp.sum(pa, axis=-1, keepdims=True)
    acc_a = al_a[:, :1] * acc_s[...] + jnp.dot(pa.astype(BF16), heads_to_lanes(vbuf, 0),
                                               preferred_element_type=F32)
    m_b = jnp.maximum(m_a, jnp.max(sb, axis=-1, keepdims=True))
    al_b = jnp.exp2(m_a - m_b)
    pb = jnp.exp2(sb - m_b[:, :1])
    l_s[...] = al_b * l_a + jnp.sum(pb, axis=-1, keepdims=True)
    acc_s[...] = al_b[:, :1] * acc_a + jnp.dot(pb.astype(BF16), heads_to_lanes(vbuf, half),
                                               preferred_element_type=F32)
    m_s[...] = m_b

    @pl.when(ci == chunks_per_seq - 1)
    def _():
        nt = o_ref.shape[0]
        nk = kn_ref.shape[0]
        r = lax.broadcasted_iota(jnp.int32, (rows, nk), 0)
        col = lax.broadcasted_iota(jnp.int32, (rows, nk), 1)
        update(kn_ref[...].astype(BF16), vn_ref[...].astype(BF16), col <= r % nt)
        o = acc_s[...] / l_s[...][:, :1]
        rh = lax.broadcasted_iota(jnp.int32, o.shape, 0) // (2 * nt)
        lh = lax.broadcasted_iota(jnp.int32, o.shape, 1) // DA_DV
        o = jnp.where(rh == lh, o, 0.0)
        red = jnp.sum(o.reshape(DA_HEADS, 2 * nt, DA_V_W), axis=0)
        lam = _diff_lambda(lp_ref, lam_init)
        out = red[:nt] - lam * red[nt:]
        gain = gain_ref[...]
        for hd in range(DA_HEADS):
            sl = slice(hd * DA_DV, (hd + 1) * DA_DV)
            o_ref[:, sl] = _subln(out[:, sl], gain, lam_init)

    @pl.when(c == nchunk - 1)
    def _():
        for k in range(1, ahead + 1):
            for d in copies(c, (c + k) % PAGE_SLOTS):
                d.wait()


def _attn_kernel(pt_ref, lp_ref, gain_ref, q_ref, k_ref, v_ref, wt_ref, kn_ref, vn_ref, ck_hbm, cv_hbm,
                 op_ref, os_ref, kb, vt, kbuf, vbuf, sem, m_s, l_s, acc_s,
                 *, lam_init, chunks_per_seq, chunks_per_step, page):
    step = ((pl.program_id(0) * pl.num_programs(1) + pl.program_id(1)) * pl.num_programs(2)
            + pl.program_id(2))
    nchunk = pl.num_programs(0) * pl.num_programs(1) * pl.num_programs(2) * chunks_per_step

    def chunk(i, carry):
        _sample_attention_chunk(step * chunks_per_step + i, nchunk, pt_ref, lp_ref, wt_ref, kn_ref, vn_ref,
                                gain_ref, ck_hbm, cv_hbm, os_ref, kbuf, vbuf, sem, m_s, l_s, acc_s,
                                lam_init, chunks_per_seq, page)
        return carry

    lax.fori_loop(0, chunks_per_step, chunk, 0)
    _prompt_attention_tile(pl.program_id(2), lp_ref, q_ref, k_ref, v_ref, gain_ref, op_ref, kb, vt, lam_init)


def _attention(page_table, lam_par, gain, q, k, v, wt, k_new, v_new, cache_k, cache_v, lam_init):
    nb, s, _ = q.shape
    bq = ATTN_BQ
    nh = ATTN_HEADS_PER_STEP
    grid = (nb, DA_HEADS // nh, s // bq)
    n_seq, n_pages = page_table.shape
    page = cache_k.shape[1] // DA_HEADS
    kw = DA_HEADS * cache_k.shape[2]
    rows = wt.shape[1]
    nt = rows // (2 * DA_HEADS)
    nk = k_new.shape[1]
    cps = n_pages // PAGES_PER_STEP
    n_steps = grid[0] * grid[1] * grid[2]
    chunks_per_step = n_seq * cps // n_steps
    assert chunks_per_step * n_steps == n_seq * cps and cps % chunks_per_step == 0
    steps_per_seq = cps // chunks_per_step
    keys = PAGES_PER_STEP * page

    def seq_of(b, h, i):
        return ((b * grid[1] + h) * grid[2] + i) // steps_per_seq

    kv_spec = pl.BlockSpec((None, s, nh * HEAD_W), lambda b, h, i, pt: (b, 0, h))
    q_spec = pl.BlockSpec((None, bq, nh * HEAD_W), lambda b, h, i, pt: (b, i, h))
    grid_spec = pltpu.PrefetchScalarGridSpec(
        num_scalar_prefetch=1,
        grid=grid,
        in_specs=[
            pl.BlockSpec((8, HEAD_W), lambda b, h, i, pt: (0, 0)),
            pl.BlockSpec((1, DA_DV), lambda b, h, i, pt: (0, 0)),
            q_spec, kv_spec, kv_spec,
            pl.BlockSpec((None, rows, kw), lambda b, h, i, pt: (seq_of(b, h, i), 0, 0)),
            pl.BlockSpec((None, nk, kw), lambda b, h, i, pt: (seq_of(b, h, i), 0, 0)),
            pl.BlockSpec((None, nk, kw), lambda b, h, i, pt: (seq_of(b, h, i), 0, 0)),
            pl.BlockSpec(memory_space=pl.ANY),
            pl.BlockSpec(memory_space=pl.ANY),
        ],
        out_specs=[q_spec, pl.BlockSpec((None, nt, kw), lambda b, h, i, pt: (seq_of(b, h, i), 0, 0))],
        scratch_shapes=[
            pltpu.VMEM((s, nh * HEAD_W), BF16), pltpu.VMEM((s // bq, nh * DA_DV, bq), BF16),
            pltpu.VMEM((PAGE_SLOTS, keys * DA_HEADS, HEAD_W), F32),
            pltpu.VMEM((PAGE_SLOTS, keys * DA_HEADS, DA_DV), F32),
            pltpu.SemaphoreType.DMA((2, PAGE_SLOTS)),
            pltpu.VMEM((rows, HEAD_W), F32), pltpu.VMEM((rows, HEAD_W), F32), pltpu.VMEM((rows, kw), F32),
        ],
    )
    return pl.pallas_call(
        functools.partial(_attn_kernel, lam_init=lam_init, chunks_per_seq=cps, chunks_per_step=chunks_per_step,
                          page=page),
        grid_spec=grid_spec,
        out_shape=[jax.ShapeDtypeStruct((nb, s, DA_V_W), BF16), jax.ShapeDtypeStruct((n_seq, nt, kw), F32)],
        compiler_params=_cparams(("arbitrary", "arbitrary", "arbitrary")),
        name="attn",
    )(page_table.reshape(-1), lam_par, gain, q, k, v, wt, k_new, v_new, cache_k, cache_v)


def _memkv_kernel(x_ref, g_ref, w_ref, k_ref, v_ref):
    h = _rms_bf16(x_ref[...], g_ref[...])
    k_ref[...] = jnp.dot(h, w_ref[:, :XA_W], preferred_element_type=F32)
    v_ref[...] = jnp.dot(h, w_ref[:, XA_W:], preferred_element_type=F32)


def _memkv(x, gain, w):
    t = x.shape[0]
    tm = min(TOKEN_TILE, t)

    def row(wd):
        return pl.BlockSpec((tm, wd), lambda i: (i, 0))

    return pl.pallas_call(
        _memkv_kernel,
        grid=(t // tm,),
        in_specs=[row(D_MODEL), _resident((1, D_MODEL)), _resident((D_MODEL, 2 * XA_W))],
        out_specs=[row(XA_W), row(XA_W)],
        out_shape=[jax.ShapeDtypeStruct((t, XA_W), F32)] * 2,
        compiler_params=_cparams(("arbitrary",)),
        name="memkv",
    )(x, gain, w)


def _xattn_kernel(q_ref, mk_ref, mv_ref, o_ref, *, head_rows):
    def mem(ref, b, hd):
        if head_rows:
            n = ref.shape[1] // XA_HEADS
            return ref[b, pl.ds(hd, n, stride=XA_HEADS), :].astype(BF16)
        return ref[b, :, hd * XA_DH:(hd + 1) * XA_DH].astype(BF16)

    pairs = [(b, hd) for b in range(q_ref.shape[0]) for hd in range(XA_HEADS)]
    qs = [q_ref[b].astype(BF16) for b in range(q_ref.shape[0])]
    scores = [lax.dot_general(qs[b][:, hd * XA_DH:(hd + 1) * XA_DH], mem(mk_ref, b, hd), _NT,
                              preferred_element_type=F32) * (XA_DH ** -0.5) for b, hd in pairs]
    probs = []
    for s in scores:
        p = jnp.exp(s - jnp.max(s, axis=-1, keepdims=True))
        probs.append((p / jnp.sum(p, axis=-1, keepdims=True)).astype(BF16))
    for (b, hd), p in zip(pairs, probs):
        o_ref[b, :, hd * XA_DH:(hd + 1) * XA_DH] = jnp.dot(
            p, mem(mv_ref, b, hd), preferred_element_type=F32).astype(o_ref.dtype)


def _xattn(xq, mk, mv, head_rows, seqs_per_step):
    nb, lq, _ = xq.shape
    tq = min(TOKEN_TILE, lq)
    ns = seqs_per_step
    q_spec = pl.BlockSpec((ns, tq, XA_W), lambda b, i: (b, i, 0))
    m_spec = pl.BlockSpec((ns,) + mk.shape[1:], lambda b, i: (b, 0, 0))
    return pl.pallas_call(
        functools.partial(_xattn_kernel, head_rows=head_rows),
        grid=(nb // ns, lq // tq),
        in_specs=[q_spec, m_spec, m_spec],
        out_specs=q_spec,
        out_shape=jax.ShapeDtypeStruct((nb, lq, XA_W), BF16),
        compiler_params=_cparams(("arbitrary", "arbitrary")),
        name="xattn",
    )(xq, mk, mv)


def _merge_kernel(x_ref, g_ref, yp_ref, yd_ref, ym_ref, wgate_ref, wbp_ref, wbd_ref, wbm_ref, wo_ref, o_ref, h_scr):
    h_scr[...] = _rms_bf16(x_ref[...], g_ref[...])
    merged = None
    for i, (y_ref, wb_ref) in enumerate(((yp_ref, wbp_ref), (yd_ref, wbd_ref), (ym_ref, wbm_ref))):
        g0 = wgate_ref.shape[1] - (N_BRANCH - i) * D_MODEL
        z = jnp.dot(h_scr[...], wgate_ref[:, g0:g0 + D_MODEL], preferred_element_type=F32)
        br = jnp.dot(y_ref[...].astype(BF16), wb_ref[...], preferred_element_type=F32)
        term = _sigmoid(z) * br
        merged = term if merged is None else merged + term
    o_ref[...] = x_ref[...] + jnp.dot(merged.astype(BF16), wo_ref[...], preferred_element_type=F32)


def _merge(x, gain, yp, yd, ym, wgate, wbp, wbd, wbm, wo):
    t = x.shape[0]
    tm = min(TOKEN_TILE, t)

    def row(wd):
        return pl.BlockSpec((tm, wd), lambda i: (i, 0))

    return pl.pallas_call(
        _merge_kernel,
        grid=(t // tm,),
        in_specs=[row(D_MODEL), _resident((1, D_MODEL)), row(POOL_W), row(DA_V_W), row(XA_W),
                  _resident(wgate.shape), _resident((POOL_W, D_MODEL)),
                  _resident((DA_V_W, D_MODEL)), _resident((XA_W, D_MODEL)), _resident((D_MODEL, D_MODEL))],
        out_specs=row(D_MODEL),
        out_shape=jax.ShapeDtypeStruct((t, D_MODEL), F32),
        scratch_shapes=[pltpu.VMEM((tm, D_MODEL), BF16)],
        compiler_params=_cparams(("arbitrary",)),
        name="merge",
    )(x, gain, yp, yd, ym, wgate, wbp, wbd, wbm, wo)


def kernel(x_prompt, x_sample, mem_prompt, cache_k, cache_v, state_pool, cache_mem_k, cache_mem_v, page_table, ffn1_norm, ffn1_w_gate, ffn1_w_up, ffn1_w_down, mix_norm, w_in, pool_w, pool_scale, lambda_q1, lambda_k1, lambda_q2, lambda_k2, subln_gain, mem_norm, w_mem_kv, w_br_pool, w_br_diff, w_br_mem, w_out, ffn2_norm, ffn2_w_gate, ffn2_w_up, ffn2_w_down, final_norm):
    depth = ffn1_norm.shape[0]
    assert depth == 1, "kernel is written for the single-layer configuration"
    bp, sp, _ = x_prompt.shape
    bs, ts, _ = x_sample.shape
    n_pool, page, _, _ = cache_k.shape[1:]
    past_len = page_table.shape[1] * page
    mem_len = mem_prompt.shape[1]
    li = 0
    lam_init = 0.8 - 0.6 * math.exp(-0.3 * li)

    def vec(g):
        return g.reshape(1, -1).astype(F32)

    w_proj = w_gate = w_in[li].astype(BF16)
    ffn1 = (vec(ffn1_norm[li]), ffn1_w_gate[li].astype(BF16), ffn1_w_up[li].astype(BF16), ffn1_w_down[li].astype(BF16))
    ffn2 = (vec(ffn2_norm[li]), ffn2_w_gate[li].astype(BF16), ffn2_w_up[li].astype(BF16), ffn2_w_down[li].astype(BF16))
    merge_w = (w_gate, w_br_pool[li].astype(BF16), w_br_diff[li].astype(BF16), w_br_mem[li].astype(BF16),
               w_out[li].astype(BF16))
    pw = pool_w[li].astype(BF16)
    ps = vec(pool_scale[li])
    sub_g = vec(subln_gain[li])
    lam_par = jnp.zeros((8, HEAD_W), F32).at[:4, :DA_DH].set(
        jnp.stack([lambda_q1[li], lambda_k1[li], lambda_q2[li], lambda_k2[li]]).astype(F32))
    fin = vec(final_norm)

    xp = x_prompt.reshape(bp * sp, D_MODEL)
    xp1 = _ffn(xp, *ffn1)
    tabs_p = _rope_tables(jnp.arange(sp))
    u_p, q_p, k_p, v_p, xq_p = _inproj(xp1, vec(mix_norm[li]), w_proj, tabs_p, BF16)
    u_p3 = u_p.reshape(bp, sp, POOL_W)
    yp_p = _pool(u_p3, pw, ps, 0, sp, 0, 1).reshape(bp * sp, POOL_W)
    mk_p, mv_p = _memkv(mem_prompt.reshape(bp * mem_len, D_MODEL), vec(mem_norm[li]), w_mem_kv[li].astype(BF16))
    ym_p = _xattn(xq_p.reshape(bp, sp, XA_W), mk_p.reshape(bp, mem_len, XA_W),
                  mv_p.reshape(bp, mem_len, XA_W), False, 1).reshape(bp * sp, XA_W)

    xs = x_sample.reshape(bs * ts, D_MODEL)
    xs1 = _ffn(xs, *ffn1)
    tabs_s = _rope_tables(jnp.tile(past_len + jnp.arange(ts), bs))
    u_s, q_s, k_s, v_s, xq_s = _inproj(xs1, vec(mix_norm[li]), w_proj, tabs_s, F32)
    u_s3 = u_s.reshape(bs, ts, POOL_W)
    lx_s = 2 * POOL_PAD
    u_ext = jnp.concatenate([jnp.zeros((bs, POOL_PAD - POOL_HIST, POOL_W), F32), state_pool[li].astype(F32), u_s3,
                             jnp.zeros((bs, lx_s - POOL_PAD - ts, POOL_W), F32)], axis=1)
    yp_s = _pool(u_ext, pw, ps, POOL_PAD, POOL_PAD, past_len - POOL_PAD,
                 SAMPLE_SEQS_PER_STEP)[:, :ts].reshape(bs * ts, POOL_W)
    q_s3 = q_s.reshape(bs, ts, DA_QK_W)
    rows = 2 * DA_HEADS * ts
    r_blk = jnp.arange(rows)[:, None] // ts
    l_blk = jnp.arange(DA_QK_W)[None, :] // DA_DH
    wt = jnp.where((r_blk == l_blk)[None], jnp.tile(q_s3, (1, 2 * DA_HEADS, 1)), 0.0).astype(BF16)
    pad_new = ((0, 0), (0, NEW_KEY_ROWS - ts), (0, 0))
    k_s3 = k_s.reshape(bs, ts, DA_QK_W)
    v_s3 = v_s.reshape(bs, ts, DA_V_W)
    yd_p, yd_s = _attention(page_table, lam_par, sub_g, q_p.reshape(bp, sp, DA_QK_W), k_p.reshape(bp, sp, DA_QK_W),
                            v_p.reshape(bp, sp, DA_V_W), wt, jnp.pad(k_s3, pad_new), jnp.pad(v_s3, pad_new),
                            cache_k[li].reshape(n_pool, page * DA_HEADS, HEAD_W),
                            cache_v[li].reshape(n_pool, page * DA_HEADS, DA_DV), lam_init)
    yd_p = yd_p.reshape(bp * sp, DA_V_W)
    yd_s = yd_s.reshape(bs * ts, DA_V_W)
    xp2 = _merge(xp1, vec(mix_norm[li]), yp_p, yd_p, ym_p, *merge_w)
    y_prompt = _ffn(xp2, *ffn2, final_gain=fin).reshape(bp, sp, D_MODEL)
    xq_pad = jnp.pad(xq_s.reshape(bs, ts, XA_W), ((0, 0), (0, 8 - ts), (0, 0)))
    ym_s = _xattn(xq_pad, cache_mem_k[li].reshape(bs, mem_len * XA_HEADS, XA_DH),
                  cache_mem_v[li].reshape(bs, mem_len * XA_HEADS, XA_DH), True,
                  SAMPLE_SEQS_PER_STEP)[:, :ts].reshape(bs * ts, XA_W)
    xs2 = _merge(xs1, vec(mix_norm[li]), yp_s, yd_s, ym_s, *merge_w)
    y_sample = _ffn(xs2, *ffn2, final_gain=fin).reshape(bs, ts, D_MODEL)

    k_prompt = k_p.reshape(1, bp, sp, DA_HEADS, 2 * DA_DH)
    v_prompt = v_p.reshape(1, bp, sp, DA_HEADS, DA_DV)
    pool_prompt = u_p3[:, sp - POOL_HIST:][None]
    mem_k_prompt = mk_p.reshape(1, bp, mem_len, XA_HEADS, XA_DH)
    mem_v_prompt = mv_p.reshape(1, bp, mem_len, XA_HEADS, XA_DH)
    k_sample = k_s.reshape(1, bs, ts, DA_HEADS, 2 * DA_DH)
    v_sample = v_s.reshape(1, bs, ts, DA_HEADS, DA_DV)
    pool_sample = u_ext[:, POOL_PAD + ts - POOL_HIST:POOL_PAD + ts][None]
    return (y_prompt, y_sample, k_prompt, v_prompt, pool_prompt, mem_k_prompt, mem_v_prompt, k_sample, v_sample,
            pool_sample)
```
